```python
import math
import jax, jax.numpy as jnp
from jax import lax
import numpy as np

D_MODEL = 1024
BATCH = 4
SEQ = 4096
DEPTH = 4

N_MIXERS = 3
EXPAND = 2
D_INNER = EXPAND * D_MODEL
GLA_HEADS = 4
GLA_DK = D_MODEL // GLA_HEADS
GLA_DV = D_INNER // GLA_HEADS
GLA_RANK = 16
GLA_GATE_NORM = 16.0
GLA_CHUNK = 64
DSA_HEADS = 16
DSA_HEAD_DIM = D_INNER // DSA_HEADS
IDX_HEADS = 8
IDX_DIM = 64
TOPK_MAX = 256
Q_BLOCK = 128
ATTN_SCALE = DSA_HEAD_DIM ** -0.5
INDEX_SCALE = (IDX_HEADS ** -0.5) * (IDX_DIM ** -0.5)
CONV_WIDTH = 31
ROPE_THETA = 500000.0
DSA_ROT_DIM = DSA_HEAD_DIM // 4
IDX_ROT_DIM = IDX_DIM // 4
PLE_DIM = 256
ALPHA = (2 * DEPTH) ** 0.25
BETA = (8 * DEPTH) ** -0.25
NORM_EPS = 1e-5

GLA_IN = 2 * GLA_HEADS * GLA_DK + 2 * D_INNER + GLA_RANK
DSA_IN = 2 * D_INNER + 2 * DSA_HEAD_DIM + IDX_HEADS * IDX_DIM + IDX_DIM + IDX_HEADS
CONV_IN = 3 * D_INNER

kernel_name = 'hybrid_gla_dsa_conformer_deepnorm'


def layer_norm(x, g, b):
    xf = x.astype(jnp.float32)
    mu = jnp.mean(xf, axis=-1, keepdims=True)
    var = jnp.mean(jnp.square(xf - mu), axis=-1, keepdims=True)
    return ((xf - mu) * lax.rsqrt(var + NORM_EPS) * g + b).astype(x.dtype)


def rope_tables(positions, rot_dim):
    inv_freq = ROPE_THETA ** (-jnp.arange(0, rot_dim, 2, dtype=jnp.float32) / rot_dim)
    ang = positions.astype(jnp.float32)[..., None] * inv_freq
    return jnp.cos(ang), jnp.sin(ang)


def apply_partial_rope(x, cos, sin):
    half = cos.shape[-1]
    rot = 2 * half
    bshape = cos.shape[:2] + (1,) * (x.ndim - 3) + (half,)
    c = cos.reshape(bshape)
    s = sin.reshape(bshape)
    x1 = x[..., :half].astype(jnp.float32)
    x2 = x[..., half:rot].astype(jnp.float32)
    rotated = jnp.concatenate([x1 * c - x2 * s, x2 * c + x1 * s], axis=-1).astype(x.dtype)
    return jnp.concatenate([rotated, x[..., rot:]], axis=-1)


def gla_mixer(x, w_in, w_a2, b_a, gn_g, w_out):
    B, L, _ = x.shape
    n = L // GLA_CHUNK
    qk = GLA_HEADS * GLA_DK
    h = x @ w_in
    q = h[..., :qk]
    k = h[..., qk:2 * qk]
    v = h[..., 2 * qk:2 * qk + D_INNER]
    z = h[..., 2 * qk + D_INNER:2 * qk + 2 * D_INNER]
    a = h[..., 2 * qk + 2 * D_INNER:]
    g = jax.nn.log_sigmoid((a @ w_a2 + b_a).astype(jnp.float32)) / GLA_GATE_NORM

    def chunk(t, d):
        return t.astype(jnp.float32).reshape(B, n, GLA_CHUNK, GLA_HEADS, d)

    q, k, g, v = chunk(q, GLA_DK), chunk(k, GLA_DK), chunk(g, GLA_DK), chunk(v, GLA_DV)
    G = jnp.cumsum(g, axis=2)
    G_last = G[:, :, -1:]
    q_t = q * (GLA_DK ** -0.5) * jnp.exp(G)
    k_t = k * jnp.exp(-G)
    k_s = k * jnp.exp(G_last - G)
    pos = jnp.arange(GLA_CHUNK)
    tril = pos[:, None] >= pos[None, :]
    A = jnp.where(tril, jnp.einsum('bnihd,bnjhd->bnhij', q_t, k_t), 0.0)
    o_intra = jnp.einsum('bnhij,bnjhv->bnihv', A, v)

    def step(S, inp):
        q_c, k_c, v_c, decay = inp
        o_c = jnp.einsum('bihd,bhdv->bihv', q_c, S)
        S = S * decay[..., None] + jnp.einsum('bjhd,bjhv->bhdv', k_c, v_c)
        return S, o_c

    S0 = jnp.zeros((B, GLA_HEADS, GLA_DK, GLA_DV), jnp.float32)
    xs = (jnp.moveaxis(q_t, 1, 0), jnp.moveaxis(k_s, 1, 0), jnp.moveaxis(v, 1, 0),
          jnp.moveaxis(jnp.exp(G_last[:, :, 0]), 1, 0))
    _, o_inter = lax.scan(step, S0, xs)
    o = o_intra + jnp.moveaxis(o_inter, 0, 1)
    o = o * lax.rsqrt(jnp.mean(o * o, axis=-1, keepdims=True) + NORM_EPS)
    o = o.reshape(B, L, D_INNER) * gn_g
    return (o.astype(x.dtype) * jax.nn.silu(z)) @ w_out


def dsa_mixer(x, cos_h, sin_h, cos_i, sin_i, w_in, w_out):
    B, L, _ = x.shape
    topk = min(TOPK_MAX, L // 4)
    nb = L // Q_BLOCK
    h = x @ w_in
    o1 = D_INNER
    o2 = o1 + DSA_HEAD_DIM
    o3 = o2 + DSA_HEAD_DIM
    o4 = o3 + D_INNER
    o5 = o4 + IDX_HEADS * IDX_DIM
    o6 = o5 + IDX_DIM
    q = apply_partial_rope(h[..., :o1].reshape(B, L, DSA_HEADS, DSA_HEAD_DIM), cos_h, sin_h)
    k = apply_partial_rope(h[..., o1:o2], cos_h, sin_h)
    v = h[..., o2:o3]
    z = h[..., o3:o4]
    qi = apply_partial_rope(h[..., o4:o5].reshape(B, L, IDX_HEADS, IDX_DIM), cos_i, sin_i)
    ki = apply_partial_rope(h[..., o5:o6], cos_i, sin_i)
    wi = h[..., o6:]
    gather = jax.vmap(lambda arr, ids: arr[ids])
    s_pos = jnp.arange(L, dtype=jnp.int32)

    def to_blocks(t):
        return jnp.moveaxis(t.reshape((B, nb, Q_BLOCK) + t.shape[2:]), 1, 0)

    def attend_block(args):
        q_b, qi_b, wi_b, start = args
        t_pos = start + jnp.arange(Q_BLOCK, dtype=jnp.int32)
        rel = jax.nn.relu(jnp.einsum('bthd,bsd->bths', qi_b, ki).astype(jnp.float32))
        score = jnp.einsum('bths,bth->bts', rel, wi_b.astype(jnp.float32)) * INDEX_SCALE
        causal = s_pos[None, :] <= t_pos[:, None]
        score = jnp.where(causal[None], score, -jnp.inf)
        _, idx = lax.top_k(score, topk)
        valid = idx <= t_pos[None, :, None]
        k_sel = gather(k, idx)
        v_sel = gather(v, idx)
        logits = jnp.einsum('bthd,btjd->bthj', q_b, k_sel).astype(jnp.float32) * ATTN_SCALE
        logits = jnp.where(valid[:, :, None, :], logits, -jnp.inf)
        probs = jax.nn.softmax(logits, axis=-1).astype(v.dtype)
        return jnp.einsum('bthj,btjd->bthd', probs, v_sel)

    starts = jnp.arange(nb, dtype=jnp.int32) * Q_BLOCK
    o = lax.map(attend_block, (to_blocks(q), to_blocks(qi), to_blocks(wi), starts))
    o = jnp.moveaxis(o, 0, 1).reshape(B, L, D_INNER)
    return (o * jax.nn.silu(z)) @ w_out


def conformer_conv_mixer(x, w_in, dw_w, dw_b, ln_g, ln_b, w_out):
    h = x @ w_in
    a, gate, z = jnp.split(h, 3, axis=-1)
    u = a * jax.nn.sigmoid(gate)
    u = lax.conv_general_dilated(u, dw_w[:, None, :], window_strides=(1,),
                                 padding=[(CONV_WIDTH - 1, 0)],
                                 dimension_numbers=('NWC', 'WIO', 'NWC'),
                                 feature_group_count=D_INNER) + dw_b
    u = jax.nn.silu(layer_norm(u, ln_g, ln_b))
    return (u * jax.nn.silu(z)) @ w_out


def setup_inputs(seed: int = 0) -> dict:
    key = jax.random.key(seed)
    ks = jax.random.split(key, 24)
    n_gla = len(range(0, DEPTH, N_MIXERS))
    n_dsa = len(range(1, DEPTH, N_MIXERS))
    n_conv = len(range(2, DEPTH, N_MIXERS))

    def normal(k, shape, scale):
        return jax.random.normal(k, shape, jnp.float32) * scale

    x = normal(ks[0], (BATCH, SEQ, D_MODEL), 1.0)
    p = normal(ks[1], (DEPTH, BATCH, SEQ, PLE_DIM), 1.0)
    offsets = jax.random.randint(ks[2], (BATCH, 1), 0, 1024, dtype=jnp.int32)
    positions = offsets + jnp.arange(SEQ, dtype=jnp.int32)[None, :]
    return {
        'x': x,
        'p': p,
        'positions': positions,
        'gla_w_in': normal(ks[3], (n_gla, D_MODEL, GLA_IN), D_MODEL ** -0.5),
        'gla_w_a2': normal(ks[4], (n_gla, GLA_RANK, GLA_HEADS * GLA_DK), GLA_RANK ** -0.5),
        'gla_b_a': normal(ks[5], (n_gla, GLA_HEADS * GLA_DK), 0.1),
        'gla_gn_g': 1.0 + normal(ks[6], (n_gla, D_INNER), 0.02),
        'gla_w_out': normal(ks[7], (n_gla, D_INNER, D_MODEL), BETA * D_INNER ** -0.5),
        'dsa_w_in': normal(ks[8], (n_dsa, D_MODEL, DSA_IN), D_MODEL ** -0.5),
        'dsa_w_out': normal(ks[9], (n_dsa, D_INNER, D_MODEL), BETA * D_INNER ** -0.5),
        'conv_w_in': normal(ks[10], (n_conv, D_MODEL, CONV_IN), D_MODEL ** -0.5),
        'conv_dw_w': normal(ks[11], (n_conv, CONV_WIDTH, D_INNER), CONV_WIDTH ** -0.5),
        'conv_dw_b': normal(ks[12], (n_conv, D_INNER), 0.02),
        'conv_ln_g': 1.0 + normal(ks[13], (n_conv, D_INNER), 0.02),
        'conv_ln_b': normal(ks[14], (n_conv, D_INNER), 0.02),
        'conv_w_out': normal(ks[15], (n_conv, D_INNER, D_MODEL), BETA * D_INNER ** -0.5),
        'ln_g': 1.0 + normal(ks[16], (DEPTH, D_MODEL), 0.02),
        'ln_b': normal(ks[17], (DEPTH, D_MODEL), 0.02),
        'ple_w': normal(ks[18], (DEPTH, PLE_DIM, D_MODEL), 0.5 * PLE_DIM ** -0.5),
        'ple_gate_w': normal(ks[19], (DEPTH, D_MODEL, D_MODEL), D_MODEL ** -0.5),
    }


def reference(x, p, positions, gla_w_in, gla_w_a2, gla_b_a, gla_gn_g, gla_w_out,
              dsa_w_in, dsa_w_out, conv_w_in, conv_dw_w, conv_dw_b, conv_ln_g, conv_ln_b,
              conv_w_out, ln_g, ln_b, ple_w, ple_gate_w):
    cos_h, sin_h = rope_tables(positions, DSA_ROT_DIM)
    cos_i, sin_i = rope_tables(positions, IDX_ROT_DIM)
    for i in range(DEPTH):
        kind = i % N_MIXERS
        j = i // N_MIXERS
        if kind == 0:
            y = gla_mixer(x, gla_w_in[j], gla_w_a2[j], gla_b_a[j], gla_gn_g[j], gla_w_out[j])
        elif kind == 1:
            y = dsa_mixer(x, cos_h, sin_h, cos_i, sin_i, dsa_w_in[j], dsa_w_out[j])
        else:
            y = conformer_conv_mixer(x, conv_w_in[j], conv_dw_w[j], conv_dw_b[j],
                                     conv_ln_g[j], conv_ln_b[j], conv_w_out[j])
        x = layer_norm(ALPHA * x + y, ln_g[i], ln_b[i])
        x = x + (p[i] @ ple_w[i]) * jax.nn.sigmoid(x @ ple_gate_w[i])
    return x
```

```python
import functools

import jax
import jax.numpy as jnp
from jax import lax
from jax.experimental import pallas as pl
from jax.experimental.pallas import tpu as pltpu

F32 = jnp.float32
BF16 = jnp.bfloat16

D_MODEL = 1024
N_MIXERS = 3
D_INNER = 2 * D_MODEL
GLA_HEADS = 4
GLA_DK = D_MODEL // GLA_HEADS
GLA_DV = D_INNER // GLA_HEADS
GLA_RANK = 16
GLA_GATE_NORM = 16.0
GLA_CHUNK = 64
DSA_HEADS = 16
DSA_HEAD_DIM = D_INNER // DSA_HEADS
IDX_HEADS = 8
IDX_DIM = 64
TOPK_MAX = 256
ATTN_SCALE = DSA_HEAD_DIM ** -0.5
INDEX_SCALE = (IDX_HEADS ** -0.5) * (IDX_DIM ** -0.5)
CONV_WIDTH = 31
ROPE_THETA = 500000.0
DSA_ROT_DIM = DSA_HEAD_DIM // 4
IDX_ROT_DIM = IDX_DIM // 4
PLE_DIM = 256
NORM_EPS = 1e-5

LANES = 128
LOG2E = 1.4426950408889634
INT_MIN = -(2 ** 31)
VMEM_LIMIT = 56 * 1024 * 1024

NT_DIMS = (((1,), (1,)), ((), ()))
TN_DIMS = (((0,), (0,)), ((), ()))


def _params(*sem):
    return pltpu.CompilerParams(dimension_semantics=sem, vmem_limit_bytes=VMEM_LIMIT)


def _dot(a, b):
    return jnp.dot(a, b, preferred_element_type=F32)


def _silu(x):
    return x * jax.nn.sigmoid(x)


def _mm_kernel(x_ref, w_ref, o_ref):
    o_ref[...] = _dot(x_ref[...].astype(BF16), w_ref[...]).astype(o_ref.dtype)


def _mm(x, w, out_dtype, tm=1024, tn=1024):
    n, k = x.shape
    c = w.shape[1]
    tm, tn = min(tm, n), min(tn, c)
    return pl.pallas_call(
        _mm_kernel,
        grid=(n // tm, c // tn),
        in_specs=[pl.BlockSpec((tm, k), lambda i, j: (i, 0)),
                  pl.BlockSpec((k, tn), lambda i, j: (0, j))],
        out_specs=pl.BlockSpec((tm, tn), lambda i, j: (i, j)),
        out_shape=jax.ShapeDtypeStruct((n, c), out_dtype),
        compiler_params=_params("parallel", "parallel"),
        name="proj",
    )(x, w)


def _post_kernel(alpha, u_ref, x_ref, p_ref, wout_ref, lng_ref, lnb_ref, plew_ref, gatew_ref, o_ref):
    y = _dot(u_ref[...], wout_ref[...])
    r = alpha * x_ref[...] + y
    mu = jnp.mean(r, axis=-1, keepdims=True)
    d = r - mu
    var = jnp.mean(d * d, axis=-1, keepdims=True)
    x1 = d * lax.rsqrt(var + NORM_EPS) * lng_ref[...] + lnb_ref[...]
    ple = _dot(p_ref[...].astype(BF16), plew_ref[...])
    gate = _dot(x1.astype(BF16), gatew_ref[...])
    o_ref[...] = x1 + ple * jax.nn.sigmoid(gate)


def _post(u, x, p, w_out, ln_g, ln_b, ple_w, gate_w, alpha, tm=512):
    n, d = x.shape
    tm = min(tm, n)
    row = lambda i: (i, 0)
    whole = lambda i: (0, 0)
    return pl.pallas_call(
        functools.partial(_post_kernel, alpha),
        grid=(n // tm,),
        in_specs=[pl.BlockSpec((tm, u.shape[1]), row),
                  pl.BlockSpec((tm, d), row),
                  pl.BlockSpec((tm, p.shape[1]), row),
                  pl.BlockSpec(w_out.shape, whole),
                  pl.BlockSpec((1, d), whole),
                  pl.BlockSpec((1, d), whole),
                  pl.BlockSpec(ple_w.shape, whole),
                  pl.BlockSpec(gate_w.shape, whole)],
        out_specs=pl.BlockSpec((tm, d), row),
        out_shape=jax.ShapeDtypeStruct((n, d), F32),
        compiler_params=_params("parallel"),
        name="post",
    )(u, x, p, w_out.astype(BF16), ln_g.reshape(1, d), ln_b.reshape(1, d),
      ple_w.astype(BF16), gate_w.astype(BF16))


def _split3_bf16(x):
    hi = x.astype(BF16)
    r = x - hi.astype(F32)
    mid = r.astype(BF16)
    lo = (r - mid.astype(F32)).astype(BF16)
    return hi, mid, lo


def _gla_gate_kernel(x_ref, wq_ref, wk_ref, wa_ref, wa2_ref, ba_ref,
                     qt_ref, kt_ref, ks_ref, dec_ref):
    xb = x_ref[...].astype(BF16)
    q = _dot(xb, wq_ref[...])
    k = _dot(xb, wk_ref[...])
    a = _dot(xb, wa_ref[...])
    pre = _dot(a.astype(BF16), wa2_ref[...]) + ba_ref[...]
    g = (jnp.minimum(pre, 0.0) - jnp.log(1.0 + jnp.exp(-jnp.abs(pre)))) / GLA_GATE_NORM
    ch = GLA_CHUNK
    ri = lax.broadcasted_iota(jnp.int32, (ch, ch), 0)
    ci = lax.broadcasted_iota(jnp.int32, (ch, ch), 1)
    tri = jnp.where(ri >= ci, 1.0, 0.0).astype(BF16)
    decs = []
    for c in range(x_ref.shape[0] // ch):
        sl = slice(c * ch, (c + 1) * ch)
        hi, mid, lo = _split3_bf16(g[sl])
        cum = _dot(tri, hi) + _dot(tri, mid) + _dot(tri, lo)
        last = cum[ch - 1:ch]
        qt_ref[sl, :] = (q[sl] * (GLA_DK ** -0.5) * jnp.exp(cum)).astype(BF16)
        kt_ref[sl, :] = (k[sl] * jnp.exp(-cum)).astype(BF16)
        ks_ref[sl, :] = (k[sl] * jnp.exp(last - cum)).astype(BF16)
        decs.append(jnp.exp(last))
    dec_ref[...] = jnp.concatenate(decs, axis=0)


def _gla_gate(x, wq, wk, wa, wa2, ba, tm=512):
    n, d = x.shape
    qk = wq.shape[1]
    row = lambda i: (i, 0)
    whole = lambda i: (0, 0)
    wa_p = jnp.zeros((d, LANES), BF16).at[:, :GLA_RANK].set(wa)
    wa2_p = jnp.zeros((LANES, qk), BF16).at[:GLA_RANK].set(wa2)
    act = jax.ShapeDtypeStruct((n, qk), BF16)
    return pl.pallas_call(
        _gla_gate_kernel,
        grid=(n // tm,),
        in_specs=[pl.BlockSpec((tm, d), row),
                  pl.BlockSpec(wq.shape, whole), pl.BlockSpec(wk.shape, whole),
                  pl.BlockSpec(wa_p.shape, whole), pl.BlockSpec(wa2_p.shape, whole),
                  pl.BlockSpec((1, qk), whole)],
        out_specs=[pl.BlockSpec((tm, qk), row)] * 3 + [pl.BlockSpec((tm // GLA_CHUNK, qk), row)],
        out_shape=[act, act, act, jax.ShapeDtypeStruct((n // GLA_CHUNK, qk), F32)],
        compiler_params=_params("parallel"),
        name="gla_gate",
    )(x, wq, wk, wa_p, wa2_p, ba.reshape(1, qk))


def _gla_core_kernel(qt_ref, kt_ref, ks_ref, v_ref, dec_ref, z_ref, gn_ref, o_ref, s_ref):
    @pl.when(pl.program_id(2) == 0)
    def _():
        s_ref[...] = jnp.zeros_like(s_ref)

    ch = GLA_CHUNK
    dec_t = dec_ref[...].T
    ri = lax.broadcasted_iota(jnp.int32, (ch, ch), 0)
    ci = lax.broadcasted_iota(jnp.int32, (ch, ch), 1)
    tril = ri >= ci
    for c in range(qt_ref.shape[0] // ch):
        sl = slice(c * ch, (c + 1) * ch)
        qt, kt, ks, v = qt_ref[sl, :], kt_ref[sl, :], ks_ref[sl, :], v_ref[sl, :]
        a = jnp.where(tril, lax.dot_general(qt, kt, NT_DIMS, preferred_element_type=F32), 0.0)
        s = s_ref[...]
        o = _dot(a.astype(BF16), v) + _dot(qt, s.astype(BF16))
        s_ref[...] = s * dec_t[:, c:c + 1] + lax.dot_general(ks, v, TN_DIMS, preferred_element_type=F32)
        o = o * lax.rsqrt(jnp.mean(o * o, axis=-1, keepdims=True) + NORM_EPS)
        o = o * gn_ref[...]
        o_ref[sl, :] = (o * _silu(z_ref[sl, :])).astype(BF16)


def _gla_core(qt, kt, ks, v, dec, z, gn_g, batch, tc=512):
    n = qt.shape[0]
    seq = n // batch
    tc = min(tc, seq)
    nt = seq // tc
    tok = lambda b, h, i: (b * nt + i, h)
    return pl.pallas_call(
        _gla_core_kernel,
        grid=(batch, GLA_HEADS, nt),
        in_specs=[pl.BlockSpec((tc, GLA_DK), tok)] * 3
                 + [pl.BlockSpec((tc, GLA_DV), tok),
                    pl.BlockSpec((tc // GLA_CHUNK, GLA_DK), tok),
                    pl.BlockSpec((tc, GLA_DV), tok),
                    pl.BlockSpec((1, GLA_DV), lambda b, h, i: (0, h))],
        out_specs=pl.BlockSpec((tc, GLA_DV), tok),
        out_shape=jax.ShapeDtypeStruct((n, D_INNER), BF16),
        scratch_shapes=[pltpu.VMEM((GLA_DK, GLA_DV), F32)],
        compiler_params=_params("parallel", "parallel", "arbitrary"),
        name="gla_core",
    )(qt, kt, ks, v, dec, z, gn_g.reshape(1, D_INNER))


def _gla_mixer(x, w_in, w_a2, b_a, gn_g, batch):
    qk = GLA_HEADS * GLA_DK
    w = w_in.astype(BF16)
    o_v, o_z, o_a = 2 * qk, 2 * qk + D_INNER, 2 * qk + 2 * D_INNER
    qt, kt, ks, dec = _gla_gate(x, w[:, :qk], w[:, qk:o_v], w[:, o_a:], w_a2.astype(BF16), b_a)
    v = _mm(x, w[:, o_v:o_z], BF16)
    z = _mm(x, w[:, o_z:o_a], F32)
    return _gla_core(qt, kt, ks, v, dec, z, gn_g, batch)


def _rope(x, cos, sin_lo, sin_hi, half):
    return (x * cos + pltpu.roll(x, half, 1) * sin_hi
            + pltpu.roll(x, LANES - half, 1) * sin_lo)


def _rope_tables(pos, invf, half, period):
    ang = pos * invf
    cos, sin = jnp.cos(ang), jnp.sin(ang)
    lane = lax.broadcasted_iota(jnp.int32, ang.shape, 1) % period
    sin_lo = jnp.where(lane < half, -sin, 0.0)
    sin_hi = jnp.where((lane >= half) & (lane < 2 * half), sin, 0.0)
    return cos, sin_lo, sin_hi


def _dsa_prep_kernel(x_ref, pos_ref, invh_ref, invi_ref, wq_ref, wkv_ref, wqi_ref, wki_ref, wwi_ref,
                     q_ref, k_ref, v_ref, qi_ref, ki_ref, wi_ref):
    xb = x_ref[...].astype(BF16)
    pos = pos_ref[...]
    hd = DSA_HEAD_DIM
    cos_h, slo_h, shi_h = _rope_tables(pos, invh_ref[...], DSA_ROT_DIM // 2, hd)
    cos_i, slo_i, shi_i = _rope_tables(pos, invi_ref[...], IDX_ROT_DIM // 2, IDX_DIM)
    q = _dot(xb, wq_ref[...])
    for h in range(DSA_HEADS):
        sl = slice(h * hd, (h + 1) * hd)
        q_ref[:, sl] = _rope(q[:, sl], cos_h, slo_h, shi_h, DSA_ROT_DIM // 2).astype(BF16)
    kv = _dot(xb, wkv_ref[...])
    k_ref[...] = _rope(kv[:, :hd], cos_h, slo_h, shi_h, DSA_ROT_DIM // 2).astype(BF16)
    v_ref[...] = kv[:, hd:].astype(BF16)
    qi = _dot(xb, wqi_ref[...])
    for j in range(qi.shape[1] // LANES):
        sl = slice(j * LANES, (j + 1) * LANES)
        qi_ref[:, sl] = _rope(qi[:, sl], cos_i, slo_i, shi_i, IDX_ROT_DIM // 2).astype(BF16)
    ki = _dot(xb, wki_ref[...])
    ki_ref[...] = _rope(ki, cos_i, slo_i, shi_i, IDX_ROT_DIM // 2).astype(BF16)
    wi_ref[...] = _dot(xb, wwi_ref[...])


def _lane_invfreq(rot_dim, period):
    inv = ROPE_THETA ** (-jnp.arange(0, rot_dim, 2, dtype=F32) / rot_dim)
    head = jnp.concatenate([inv, inv, jnp.zeros((period - rot_dim,), F32)])
    return jnp.tile(head, LANES // period).reshape(1, LANES)


def _pad_cols(w, width):
    return jnp.zeros((w.shape[0], width), w.dtype).at[:, :w.shape[1]].set(w)


def _dsa_prep(x, pos, wq, wkv, wqi, wki, wwi, tm=512):
    n, d = x.shape
    tm = min(tm, n)
    row = lambda i: (i, 0)
    whole = lambda i: (0, 0)
    wki_p, wwi_p = _pad_cols(wki, LANES), _pad_cols(wwi, LANES)
    ws = [wq, wkv, wqi, wki_p, wwi_p]
    return pl.pallas_call(
        _dsa_prep_kernel,
        grid=(n // tm,),
        in_specs=[pl.BlockSpec((tm, d), row), pl.BlockSpec((tm, 1), row),
                  pl.BlockSpec((1, LANES), whole), pl.BlockSpec((1, LANES), whole)]
                 + [pl.BlockSpec(w.shape, whole) for w in ws],
        out_specs=[pl.BlockSpec((tm, D_INNER), row), pl.BlockSpec((tm, LANES), row),
                   pl.BlockSpec((tm, LANES), row), pl.BlockSpec((tm, IDX_HEADS * IDX_DIM), row),
                   pl.BlockSpec((tm, LANES), row), pl.BlockSpec((tm, LANES), row)],
        out_shape=[jax.ShapeDtypeStruct((n, D_INNER), BF16), jax.ShapeDtypeStruct((n, LANES), BF16),
                   jax.ShapeDtypeStruct((n, LANES), BF16),
                   jax.ShapeDtypeStruct((n, IDX_HEADS * IDX_DIM), BF16),
                   jax.ShapeDtypeStruct((n, LANES), BF16), jax.ShapeDtypeStruct((n, LANES), F32)],
        compiler_params=_params("parallel"),
        name="dsa_prep",
    )(x, pos, _lane_invfreq(DSA_ROT_DIM, DSA_HEAD_DIM), _lane_invfreq(IDX_ROT_DIM, IDX_DIM), *ws)


def _sortable_key(score):
    b = pltpu.bitcast(score, jnp.int32)
    key = b ^ ((b >> 31) & 0x7FFFFFFF)
    return jnp.where(score == 0.0, 0, key)


def _dsa_core_kernel(topk, tq, sc, hg,
                     q_ref, qi_ref, wi_ref, z_ref, k_ref, v_ref, ki_ref, o_ref,
                     key_ref, bias_ref, qs_ref, os_ref):
    qb = pl.program_id(1)
    t0 = qb * tq
    nkc = (t0 + tq + sc - 1) // sc
    hd = DSA_HEAD_DIM
    n_sub = sc // LANES

    qi_s = jnp.concatenate([qi_ref[:, h * IDX_DIM:(h + 1) * IDX_DIM] for h in range(IDX_HEADS)], axis=0)
    wi = wi_ref[...]
    wi_s = jnp.concatenate([wi[:, h:h + 1] for h in range(IDX_HEADS)], axis=0)
    t_pos = t0 + lax.broadcasted_iota(jnp.int32, (tq, sc), 0)
    col = lax.broadcasted_iota(jnp.int32, (tq, sc), 1)

    def score_body(c, carry):
        ki_c = ki_ref[pl.ds(pl.multiple_of(c * sc, sc), sc), :][:, :IDX_DIM]
        rel = jnp.maximum(lax.dot_general(qi_s, ki_c, NT_DIMS, preferred_element_type=F32), 0.0)
        score = jnp.sum((rel * wi_s).reshape(IDX_HEADS, tq, sc), axis=0) * INDEX_SCALE
        causal = (c * sc + col) <= t_pos
        key_ref[c] = jnp.where(causal, _sortable_key(score), INT_MIN)
        return carry

    lax.fori_loop(0, nkc, score_body, 0)

    def count(pred_fn):
        def body(c, acc):
            m = pred_fn(key_ref[c], c)
            part = m[:, :LANES]
            for j in range(1, n_sub):
                part = part + m[:, j * LANES:(j + 1) * LANES]
            return acc + part
        acc = lax.fori_loop(0, nkc, body, jnp.zeros((tq, LANES), jnp.int32))
        return jnp.sum(acc, axis=-1, keepdims=True)

    def ge_count(cand):
        cb = jnp.broadcast_to(cand, (tq, sc))
        return count(lambda key, c: jnp.where(key >= cb, 1, 0))

    def thr_body(i, r):
        cand = r ^ (jnp.int32(1) << (31 - i))
        return jnp.where(ge_count(cand) >= topk, cand, r)

    thr = lax.fori_loop(0, 32, thr_body, jnp.full((tq, 1), INT_MIN, jnp.int32))
    thr_b = jnp.broadcast_to(thr, (tq, sc))
    n_gt = count(lambda key, c: jnp.where(key > thr_b, 1, 0))
    n_ge = ge_count(thr)
    need = topk - n_gt
    has_thr = thr != INT_MIN
    excess = jnp.max(jnp.where(has_thr, n_ge - n_gt - need, 0))

    def tie_limit():
        def body(i, r):
            cand = r | (jnp.int32(1) << (15 - i))
            cb = jnp.broadcast_to(cand, (tq, sc))
            n = count(lambda key, c: jnp.where((key == thr_b) & ((c * sc + col) < cb), 1, 0))
            return jnp.where(n < need, cand, r)
        return lax.fori_loop(0, 16, body, jnp.zeros((tq, 1), jnp.int32))

    limit = lax.cond(excess > 0, tie_limit, lambda: jnp.full((tq, 1), 2 ** 30, jnp.int32))
    limit_b = jnp.broadcast_to(jnp.where(has_thr, limit, -1), (tq, sc))

    def bias_body(c, carry):
        key = key_ref[c]
        sel = (key > thr_b) | ((key == thr_b) & ((c * sc + col) <= limit_b))
        bias_ref[c] = jnp.where(sel, 0.0, -jnp.inf)
        return carry

    lax.fori_loop(0, nkc, bias_body, 0)

    for h in range(DSA_HEADS):
        qs_ref[h * tq:(h + 1) * tq, :] = q_ref[:, h * hd:(h + 1) * hd]
    rows = hg * tq
    scale2 = ATTN_SCALE * LOG2E

    def group_body(g, carry):
        r0 = pl.multiple_of(g * rows, rows)
        qs = qs_ref[pl.ds(r0, rows), :]

        def chunk_body(c, mla):
            m_prev, l_prev, acc = mla
            k0 = pl.multiple_of(c * sc, sc)
            s = lax.dot_general(qs, k_ref[pl.ds(k0, sc), :], NT_DIMS, preferred_element_type=F32)
            s = (s.reshape(hg, tq, sc) + bias_ref[c][None]).reshape(rows, sc)
            m_new = jnp.maximum(m_prev, jnp.max(s, axis=-1, keepdims=True))
            m_safe = jnp.where(m_new == -jnp.inf, 0.0, m_new)
            p = jnp.exp2((s - m_safe) * scale2)
            alpha = jnp.exp2((m_prev - m_safe) * scale2)
            l_new = alpha * l_prev + jnp.sum(p, axis=-1, keepdims=True)
            acc = alpha * acc + _dot(p.astype(BF16), v_ref[pl.ds(k0, sc), :])
            return m_new, l_new, acc

        init = (jnp.full((rows, 1), -jnp.inf, F32), jnp.zeros((rows, 1), F32), jnp.zeros((rows, hd), F32))
        _, l_fin, acc = lax.fori_loop(0, nkc, chunk_body, init)
        os_ref[pl.ds(r0, rows), :] = acc / l_fin
        return carry

    lax.fori_loop(0, DSA_HEADS // hg, group_body, 0)
    for h in range(DSA_HEADS):
        sl = slice(h * hd, (h + 1) * hd)
        o_ref[:, sl] = (os_ref[h * tq:(h + 1) * tq, :] * _silu(z_ref[:, sl])).astype(BF16)


def _dsa_core(q, qi, wi, z, k, v, ki, batch, tq=128, sc=256, hg=4):
    n = q.shape[0]
    seq = n // batch
    tq, sc = min(tq, seq), min(sc, seq)
    nq = seq // tq
    topk = min(TOPK_MAX, seq // 4)
    blk = lambda b, i: (b * nq + i, 0)
    full = lambda b, i: (b, 0)
    return pl.pallas_call(
        functools.partial(_dsa_core_kernel, topk, tq, sc, hg),
        grid=(batch, nq),
        in_specs=[pl.BlockSpec((tq, D_INNER), blk), pl.BlockSpec((tq, IDX_HEADS * IDX_DIM), blk),
                  pl.BlockSpec((tq, LANES), blk), pl.BlockSpec((tq, D_INNER), blk),
                  pl.BlockSpec((seq, LANES), full), pl.BlockSpec((seq, LANES), full),
                  pl.BlockSpec((seq, LANES), full)],
        out_specs=pl.BlockSpec((tq, D_INNER), blk),
        out_shape=jax.ShapeDtypeStruct((n, D_INNER), BF16),
        scratch_shapes=[pltpu.VMEM((seq // sc, tq, sc), jnp.int32),
                        pltpu.VMEM((seq // sc, tq, sc), F32),
                        pltpu.VMEM((DSA_HEADS * tq, DSA_HEAD_DIM), BF16),
                        pltpu.VMEM((DSA_HEADS * tq, DSA_HEAD_DIM), F32)],
        compiler_params=_params("parallel", "arbitrary"),
        name="dsa_core",
    )(q, qi, wi, z, k, v, ki)


def _dsa_mixer(x, pos, w_in, batch):
    w = w_in.astype(BF16)
    hd = DSA_HEAD_DIM
    o1 = D_INNER
    o3 = o1 + 2 * hd
    o4 = o3 + D_INNER
    o5 = o4 + IDX_HEADS * IDX_DIM
    o6 = o5 + IDX_DIM
    q, k, v, qi, ki, wi = _dsa_prep(x, pos, w[:, :o1], w[:, o1:o3], w[:, o4:o5], w[:, o5:o6], w[:, o6:])
    z = _mm(x, w[:, o3:o4], F32)
    return _dsa_core(q, qi, wi, z, k, v, ki, batch)


CONV_HALO = 32


def _conv_core_kernel(a_ref, gate_ref, z_ref, w_ref, b_ref, lng_ref, lnb_ref, o_ref, buf_ref):
    t = a_ref.shape[0]

    @pl.when(pl.program_id(1) == 0)
    def _():
        buf_ref[0:CONV_HALO, :] = jnp.zeros((CONV_HALO, buf_ref.shape[1]), F32)

    buf_ref[CONV_HALO:CONV_HALO + t, :] = a_ref[...] * jax.nn.sigmoid(gate_ref[...])
    first = CONV_HALO - (CONV_WIDTH - 1)
    acc = jnp.zeros(a_ref.shape, F32)
    for j in range(CONV_WIDTH):
        acc = acc + buf_ref[first + j:first + j + t, :] * w_ref[j:j + 1, :]
    u = acc + b_ref[...]
    mu = jnp.mean(u, axis=-1, keepdims=True)
    d = u - mu
    var = jnp.mean(d * d, axis=-1, keepdims=True)
    u = _silu(d * lax.rsqrt(var + NORM_EPS) * lng_ref[...] + lnb_ref[...])
    o_ref[...] = (u * _silu(z_ref[...])).astype(BF16)
    buf_ref[0:CONV_HALO, :] = buf_ref[t:t + CONV_HALO, :]


def _conv_core(a, gate, z, dw_w, dw_b, ln_g, ln_b, batch, t=256):
    n, c = a.shape
    seq = n // batch
    t = min(t, seq)
    nt = seq // t
    blk = lambda b, i: (b * nt + i, 0)
    whole = lambda b, i: (0, 0)
    w_p = jnp.zeros((CONV_HALO, c), F32).at[:CONV_WIDTH].set(dw_w)
    vec = pl.BlockSpec((1, c), whole)
    return pl.pallas_call(
        _conv_core_kernel,
        grid=(batch, nt),
        in_specs=[pl.BlockSpec((t, c), blk)] * 3 + [pl.BlockSpec(w_p.shape, whole), vec, vec, vec],
        out_specs=pl.BlockSpec((t, c), blk),
        out_shape=jax.ShapeDtypeStruct((n, c), BF16),
        scratch_shapes=[pltpu.VMEM((t + CONV_HALO, c), F32)],
        compiler_params=_params("parallel", "arbitrary"),
        name="conv_core",
    )(a, gate, z, w_p, dw_b.reshape(1, c), ln_g.reshape(1, c), ln_b.reshape(1, c))


def _conv_mixer(x, w_in, dw_w, dw_b, ln_g, ln_b, batch):
    w = w_in.astype(BF16)
    a = _mm(x, w[:, :D_INNER], F32)
    gate = _mm(x, w[:, D_INNER:2 * D_INNER], F32)
    z = _mm(x, w[:, 2 * D_INNER:], F32)
    return _conv_core(a, gate, z, dw_w, dw_b, ln_g, ln_b, batch)


def kernel(x, p, positions, gla_w_in, gla_w_a2, gla_b_a, gla_gn_g, gla_w_out, dsa_w_in, dsa_w_out,
           conv_w_in, conv_dw_w, conv_dw_b, conv_ln_g, conv_ln_b, conv_w_out, ln_g, ln_b, ple_w,
           ple_gate_w):
    batch, seq, d = x.shape
    depth = p.shape[0]
    n = batch * seq
    alpha = (2 * depth) ** 0.25
    xf = x.reshape(n, d)
    pos = positions.astype(F32).reshape(n, 1)
    for i in range(depth):
        kind, j = i % N_MIXERS, i // N_MIXERS
        if kind == 0:
            u = _gla_mixer(xf, gla_w_in[j], gla_w_a2[j], gla_b_a[j], gla_gn_g[j], batch)
            w_out = gla_w_out[j]
        elif kind == 1:
            u = _dsa_mixer(xf, pos, dsa_w_in[j], batch)
            w_out = dsa_w_out[j]
        else:
            u = _conv_mixer(xf, conv_w_in[j], conv_dw_w[j], conv_dw_b[j], conv_ln_g[j], conv_ln_b[j], batch)
            w_out = conv_w_out[j]
        xf = _post(u, xf, p[i].reshape(n, -1), w_out, ln_g[i], ln_b[i], ple_w[i], ple_gate_w[i], alpha)
    return xf.reshape(batch, seq, d)
```

```python
import functools

import jax
import jax.numpy as jnp
from jax import lax
from jax.experimental import pallas as pl
from jax.experimental.pallas import tpu as pltpu

F32 = jnp.float32
BF16 = jnp.bfloat16

D_MODEL = 1024
N_MIXERS = 3
D_INNER = 2 * D_MODEL
GLA_HEADS = 4
GLA_DK = D_MODEL // GLA_HEADS
GLA_DV = D_INNER // GLA_HEADS
GLA_RANK = 16
GLA_GATE_NORM = 16.0
GLA_CHUNK = 64
DSA_HEADS = 16
DSA_HEAD_DIM = D_INNER // DSA_HEADS
IDX_HEADS = 8
IDX_DIM = 64
TOPK_MAX = 256
ATTN_SCALE = DSA_HEAD_DIM ** -0.5
INDEX_SCALE = (IDX_HEADS ** -0.5) * (IDX_DIM ** -0.5)
CONV_WIDTH = 31
ROPE_THETA = 500000.0
DSA_ROT_DIM = DSA_HEAD_DIM // 4
IDX_ROT_DIM = IDX_DIM // 4
PLE_DIM = 256
NORM_EPS = 1e-5

LANES = 128
LOG2E = 1.4426950408889634
INT_MIN = -(2 ** 31)
VMEM_LIMIT = 56 * 1024 * 1024

NT_DIMS = (((1,), (1,)), ((), ()))
TN_DIMS = (((0,), (0,)), ((), ()))


def _params(*sem):
    return pltpu.CompilerParams(dimension_semantics=sem, vmem_limit_bytes=VMEM_LIMIT)


def _dot(a, b):
    return jnp.dot(a, b, preferred_element_type=F32)


def _silu(x):
    return x * jax.nn.sigmoid(x)


def _mm_kernel(x_ref, w_ref, o_ref):
    o_ref[...] = _dot(x_ref[...].astype(BF16), w_ref[...]).astype(o_ref.dtype)


def _mm(x, w, out_dtype, tm=1024, tn=1024):
    n, k = x.shape
    c = w.shape[1]
    tm, tn = min(tm, n), min(tn, c)
    return pl.pallas_call(
        _mm_kernel,
        grid=(n // tm, c // tn),
        in_specs=[pl.BlockSpec((tm, k), lambda i, j: (i, 0)),
                  pl.BlockSpec((k, tn), lambda i, j: (0, j))],
        out_specs=pl.BlockSpec((tm, tn), lambda i, j: (i, j)),
        out_shape=jax.ShapeDtypeStruct((n, c), out_dtype),
        compiler_params=_params("parallel", "parallel"),
        name="proj",
    )(x, w)


def _post_kernel(alpha, u_ref, x_ref, p_ref, wout_ref, lng_ref, lnb_ref, plew_ref, gatew_ref, o_ref):
    y = _dot(u_ref[...], wout_ref[...])
    r = alpha * x_ref[...] + y
    mu = jnp.mean(r, axis=-1, keepdims=True)
    d = r - mu
    var = jnp.mean(d * d, axis=-1, keepdims=True)
    x1 = d * lax.rsqrt(var + NORM_EPS) * lng_ref[...] + lnb_ref[...]
    ple = _dot(p_ref[...].astype(BF16), plew_ref[...])
    gate = _dot(x1.astype(BF16), gatew_ref[...])
    o_ref[...] = x1 + ple * jax.nn.sigmoid(gate)


def _post(u, x, p, w_out, ln_g, ln_b, ple_w, gate_w, alpha, tm=512):
    n, d = x.shape
    tm = min(tm, n)
    row = lambda i: (i, 0)
    whole = lambda i: (0, 0)
    return pl.pallas_call(
        functools.partial(_post_kernel, alpha),
        grid=(n // tm,),
        in_specs=[pl.BlockSpec((tm, u.shape[1]), row),
                  pl.BlockSpec((tm, d), row),
                  pl.BlockSpec((tm, p.shape[1]), row),
                  pl.BlockSpec(w_out.shape, whole),
                  pl.BlockSpec((1, d), whole),
                  pl.BlockSpec((1, d), whole),
                  pl.BlockSpec(ple_w.shape, whole),
                  pl.BlockSpec(gate_w.shape, whole)],
        out_specs=pl.BlockSpec((tm, d), row),
        out_shape=jax.ShapeDtypeStruct((n, d), F32),
        compiler_params=_params("parallel"),
        name="post",
    )(u, x, p, w_out.astype(BF16), ln_g.reshape(1, d), ln_b.reshape(1, d),
      ple_w.astype(BF16), gate_w.astype(BF16))


def _split3_bf16(x):
    hi = x.astype(BF16)
    r = x - hi.astype(F32)
    mid = r.astype(BF16)
    lo = (r - mid.astype(F32)).astype(BF16)
    return hi, mid, lo


def _gla_gate_kernel(x_ref, wq_ref, wk_ref, wa_ref, wa2_ref, ba_ref,
                     qt_ref, kt_ref, ks_ref, dec_ref):
    xb = x_ref[...].astype(BF16)
    q = _dot(xb, wq_ref[...])
    k = _dot(xb, wk_ref[...])
    a = _dot(xb, wa_ref[...])
    pre = _dot(a.astype(BF16), wa2_ref[...]) + ba_ref[...]
    g = (jnp.minimum(pre, 0.0) - jnp.log(1.0 + jnp.exp(-jnp.abs(pre)))) / GLA_GATE_NORM
    ch = GLA_CHUNK
    ri = lax.broadcasted_iota(jnp.int32, (ch, ch), 0)
    ci = lax.broadcasted_iota(jnp.int32, (ch, ch), 1)
    tri = jnp.where(ri >= ci, 1.0, 0.0).astype(BF16)
    decs = []
    for c in range(x_ref.shape[0] // ch):
        sl = slice(c * ch, (c + 1) * ch)
        hi, mid, lo = _split3_bf16(g[sl])
        cum = _dot(tri, hi) + _dot(tri, mid) + _dot(tri, lo)
        last = cum[ch - 1:ch]
        qt_ref[sl, :] = (q[sl] * (GLA_DK ** -0.5) * jnp.exp(cum)).astype(BF16)
        kt_ref[sl, :] = (k[sl] * jnp.exp(-cum)).astype(BF16)
        ks_ref[sl, :] = (k[sl] * jnp.exp(last - cum)).astype(BF16)
        decs.append(jnp.exp(last))
    dec_ref[...] = jnp.concatenate(decs, axis=0)


def _gla_gate(x, wq, wk, wa, wa2, ba, tm=512):
    n, d = x.shape
    qk = wq.shape[1]
    row = lambda i: (i, 0)
    whole = lambda i: (0, 0)
    wa_p = jnp.zeros((d, LANES), BF16).at[:, :GLA_RANK].set(wa)
    wa2_p = jnp.zeros((LANES, qk), BF16).at[:GLA_RANK].set(wa2)
    act = jax.ShapeDtypeStruct((n, qk), BF16)
    return pl.pallas_call(
        _gla_gate_kernel,
        grid=(n // tm,),
        in_specs=[pl.BlockSpec((tm, d), row),
                  pl.BlockSpec(wq.shape, whole), pl.BlockSpec(wk.shape, whole),
                  pl.BlockSpec(wa_p.shape, whole), pl.BlockSpec(wa2_p.shape, whole),
                  pl.BlockSpec((1, qk), whole)],
        out_specs=[pl.BlockSpec((tm, qk), row)] * 3 + [pl.BlockSpec((tm // GLA_CHUNK, qk), row)],
        out_shape=[act, act, act, jax.ShapeDtypeStruct((n // GLA_CHUNK, qk), F32)],
        compiler_params=_params("parallel"),
        name="gla_gate",
    )(x, wq, wk, wa_p, wa2_p, ba.reshape(1, qk))


def _gla_core_kernel(qt_ref, kt_ref, ks_ref, v_ref, dec_ref, z_ref, gn_ref, o_ref, s_ref):
    @pl.when(pl.program_id(2) == 0)
    def _():
        s_ref[...] = jnp.zeros_like(s_ref)

    ch = GLA_CHUNK
    dec_t = dec_ref[...].T
    ri = lax.broadcasted_iota(jnp.int32, (ch, ch), 0)
    ci = lax.broadcasted_iota(jnp.int32, (ch, ch), 1)
    tril = ri >= ci
    for c in range(qt_ref.shape[0] // ch):
        sl = slice(c * ch, (c + 1) * ch)
        qt, kt, ks, v = qt_ref[sl, :], kt_ref[sl, :], ks_ref[sl, :], v_ref[sl, :]
        a = jnp.where(tril, lax.dot_general(qt, kt, NT_DIMS, preferred_element_type=F32), 0.0)
        s = s_ref[...]
        o = _dot(a.astype(BF16), v) + _dot(qt, s.astype(BF16))
        s_ref[...] = s * dec_t[:, c:c + 1] + lax.dot_general(ks, v, TN_DIMS, preferred_element_type=F32)
        o = o * lax.rsqrt(jnp.mean(o * o, axis=-1, keepdims=True) + NORM_EPS)
        o = o * gn_ref[...]
        o_ref[sl, :] = (o * _silu(z_ref[sl, :])).astype(BF16)


def _gla_core(qt, kt, ks, v, dec, z, gn_g, batch, tc=512):
    n = qt.shape[0]
    seq = n // batch
    tc = min(tc, seq)
    nt = seq // tc
    tok = lambda b, h, i: (b * nt + i, h)
    return pl.pallas_call(
        _gla_core_kernel,
        grid=(batch, GLA_HEADS, nt),
        in_specs=[pl.BlockSpec((tc, GLA_DK), tok)] * 3
                 + [pl.BlockSpec((tc, GLA_DV), tok),
                    pl.BlockSpec((tc // GLA_CHUNK, GLA_DK), tok),
                    pl.BlockSpec((tc, GLA_DV), tok),
                    pl.BlockSpec((1, GLA_DV), lambda b, h, i: (0, h))],
        out_specs=pl.BlockSpec((tc, GLA_DV), tok),
        out_shape=jax.ShapeDtypeStruct((n, D_INNER), BF16),
        scratch_shapes=[pltpu.VMEM((GLA_DK, GLA_DV), F32)],
        compiler_params=_params("parallel", "parallel", "arbitrary"),
        name="gla_core",
    )(qt, kt, ks, v, dec, z, gn_g.reshape(1, D_INNER))


def _gla_mixer(x, w_in, w_a2, b_a, gn_g, batch):
    qk = GLA_HEADS * GLA_DK
    w = w_in.astype(BF16)
    o_v, o_z, o_a = 2 * qk, 2 * qk + D_INNER, 2 * qk + 2 * D_INNER
    qt, kt, ks, dec = _gla_gate(x, w[:, :qk], w[:, qk:o_v], w[:, o_a:], w_a2.astype(BF16), b_a)
    v = _mm(x, w[:, o_v:o_z], BF16)
    z = _mm(x, w[:, o_z:o_a], F32)
    return _gla_core(qt, kt, ks, v, dec, z, gn_g, batch)


def _rope(x, cos, sin_lo, sin_hi, half):
    return (x * cos + pltpu.roll(x, half, 1) * sin_hi
            + pltpu.roll(x, LANES - half, 1) * sin_lo)


def _rope_tables(pos, invf, half, period):
    ang = pos * invf
    cos, sin = jnp.cos(ang), jnp.sin(ang)
    lane = lax.broadcasted_iota(jnp.int32, ang.shape, 1) % period
    sin_lo = jnp.where(lane < half, -sin, 0.0)
    sin_hi = jnp.where((lane >= half) & (lane < 2 * half), sin, 0.0)
    return cos, sin_lo, sin_hi


def _dsa_prep_kernel(x_ref, pos_ref, invh_ref, invi_ref, wq_ref, wkv_ref, wqi_ref, wki_ref, wwi_ref,
                     q_ref, k_ref, v_ref, qi_ref, ki_ref, wi_ref):
    xb = x_ref[...].astype(BF16)
    pos = pos_ref[...]
    hd = DSA_HEAD_DIM
    cos_h, slo_h, shi_h = _rope_tables(pos, invh_ref[...], DSA_ROT_DIM // 2, hd)
    cos_i, slo_i, shi_i = _rope_tables(pos, invi_ref[...], IDX_ROT_DIM // 2, IDX_DIM)
    q = _dot(xb, wq_ref[...])
    for h in range(DSA_HEADS):
        sl = slice(h * hd, (h + 1) * hd)
        q_ref[:, sl] = (_rope(q[:, sl], cos_h, slo_h, shi_h, DSA_ROT_DIM // 2)
                        * (ATTN_SCALE * LOG2E)).astype(BF16)
    kv = _dot(xb, wkv_ref[...])
    k_ref[...] = _rope(kv[:, :hd], cos_h, slo_h, shi_h, DSA_ROT_DIM // 2).astype(BF16)
    ones_col = jnp.where(lax.broadcasted_iota(jnp.int32, (x_ref.shape[0], LANES), 1) == 0, 1.0, 0.0)
    v_ref[:, :hd] = kv[:, hd:].astype(BF16)
    v_ref[:, hd:] = ones_col.astype(BF16)
    qi = _dot(xb, wqi_ref[...])
    for j in range(qi.shape[1] // LANES):
        sl = slice(j * LANES, (j + 1) * LANES)
        qi_ref[:, sl] = _rope(qi[:, sl], cos_i, slo_i, shi_i, IDX_ROT_DIM // 2).astype(BF16)
    ki = _dot(xb, wki_ref[...])
    ki_ref[...] = _rope(ki, cos_i, slo_i, shi_i, IDX_ROT_DIM // 2).astype(BF16)
    wi_ref[...] = _dot(xb, wwi_ref[...])


def _lane_invfreq(rot_dim, period):
    inv = ROPE_THETA ** (-jnp.arange(0, rot_dim, 2, dtype=F32) / rot_dim)
    head = jnp.concatenate([inv, inv, jnp.zeros((period - rot_dim,), F32)])
    return jnp.tile(head, LANES // period).reshape(1, LANES)


def _pad_cols(w, width):
    return jnp.zeros((w.shape[0], width), w.dtype).at[:, :w.shape[1]].set(w)


def _dsa_prep(x, pos, wq, wkv, wqi, wki, wwi, tm=512):
    n, d = x.shape
    tm = min(tm, n)
    row = lambda i: (i, 0)
    whole = lambda i: (0, 0)
    wki_p, wwi_p = _pad_cols(wki, LANES), _pad_cols(wwi, LANES)
    ws = [wq, wkv, wqi, wki_p, wwi_p]
    return pl.pallas_call(
        _dsa_prep_kernel,
        grid=(n // tm,),
        in_specs=[pl.BlockSpec((tm, d), row), pl.BlockSpec((tm, 1), row),
                  pl.BlockSpec((1, LANES), whole), pl.BlockSpec((1, LANES), whole)]
                 + [pl.BlockSpec(w.shape, whole) for w in ws],
        out_specs=[pl.BlockSpec((tm, D_INNER), row), pl.BlockSpec((tm, LANES), row),
                   pl.BlockSpec((tm, 2 * LANES), row), pl.BlockSpec((tm, IDX_HEADS * IDX_DIM), row),
                   pl.BlockSpec((tm, LANES), row), pl.BlockSpec((tm, LANES), row)],
        out_shape=[jax.ShapeDtypeStruct((n, D_INNER), BF16), jax.ShapeDtypeStruct((n, LANES), BF16),
                   jax.ShapeDtypeStruct((n, 2 * LANES), BF16),
                   jax.ShapeDtypeStruct((n, IDX_HEADS * IDX_DIM), BF16),
                   jax.ShapeDtypeStruct((n, LANES), BF16), jax.ShapeDtypeStruct((n, LANES), F32)],
        compiler_params=_params("parallel"),
        name="dsa_prep",
    )(x, pos, _lane_invfreq(DSA_ROT_DIM, DSA_HEAD_DIM), _lane_invfreq(IDX_ROT_DIM, IDX_DIM), *ws)


COUNT_ROWS = 64


def _sortable_key(score):
    b = pltpu.bitcast(score, jnp.int32)
    key = b ^ ((b >> 31) & 0x7FFFFFFF)
    return jnp.where(score == 0.0, 0, key)


def _dsa_core_kernel(topk, tq, sc, hg,
                     q_ref, qi_ref, wi_ref, z_ref, k_ref, va_ref, ki_ref, o_ref,
                     key_ref, bias_ref, qs_ref, os_ref, s_ref):
    qb = pl.program_id(1)
    t0 = qb * tq
    nkc = (t0 + tq + sc - 1) // sc
    hd = DSA_HEAD_DIM

    qi_s = jnp.concatenate([qi_ref[:, h * IDX_DIM:(h + 1) * IDX_DIM] for h in range(IDX_HEADS)], axis=0)
    wi_t = wi_ref[...].T
    wi_row = jnp.concatenate([wi_t[h:h + 1, :] for h in range(IDX_HEADS)], axis=1)
    k_pos = lax.broadcasted_iota(jnp.int32, (sc, tq), 0)
    t_pos = t0 + lax.broadcasted_iota(jnp.int32, (sc, tq), 1)

    def score_body(c, carry):
        ki_c = ki_ref[pl.ds(pl.multiple_of(c * sc, sc), sc), :][:, :IDX_DIM]
        rel = jnp.maximum(lax.dot_general(ki_c, qi_s, NT_DIMS, preferred_element_type=F32), 0.0) * wi_row
        score = rel[:, :tq]
        for h in range(1, IDX_HEADS):
            score = score + rel[:, h * tq:(h + 1) * tq]
        score = score * INDEX_SCALE
        key_ref[c] = jnp.where(c * sc + k_pos <= t_pos, _sortable_key(score), INT_MIN)
        return carry

    lax.fori_loop(0, nkc, score_body, 0)

    def count(pred_fn):
        def body(c, acc):
            m = jnp.where(pred_fn(key_ref[c], c), 1.0, 0.0)
            return acc + jnp.sum(m.reshape(sc // COUNT_ROWS, COUNT_ROWS, tq), axis=0)
        acc = lax.fori_loop(0, nkc, body, jnp.zeros((COUNT_ROWS, tq), F32))
        return jnp.sum(acc, axis=0, keepdims=True)

    def thr_body(i, r):
        cand = r ^ (jnp.int32(1) << (31 - i))
        return jnp.where(count(lambda key, c: key >= cand) >= topk, cand, r)

    thr = lax.fori_loop(0, 32, thr_body, jnp.full((1, tq), INT_MIN, jnp.int32))
    n_gt = count(lambda key, c: key > thr)
    n_ge = count(lambda key, c: key >= thr)
    need = topk - n_gt
    has_thr = thr != INT_MIN
    excess = jnp.max(jnp.where(has_thr, n_ge - n_gt - need, 0.0))

    def tie_limit():
        def body(i, r):
            cand = r | (jnp.int32(1) << (15 - i))
            n = count(lambda key, c: (key == thr) & (c * sc + k_pos < cand))
            return jnp.where(n < need, cand, r)
        return lax.fori_loop(0, 16, body, jnp.zeros((1, tq), jnp.int32))

    limit = lax.cond(excess > 0.0, tie_limit, lambda: jnp.full((1, tq), 2 ** 30, jnp.int32))
    limit = jnp.where(has_thr, limit, -1)

    def bias_body(c, carry):
        key = key_ref[c]
        sel = (key > thr) | ((key == thr) & (c * sc + k_pos <= limit))
        bias_ref[c] = jnp.where(sel, 0.0, -jnp.inf).T
        return carry

    lax.fori_loop(0, nkc, bias_body, 0)

    for h in range(DSA_HEADS):
        qs_ref[h * tq:(h + 1) * tq, :] = q_ref[:, h * hd:(h + 1) * hd]
    rows = hg * tq
    groups = DSA_HEADS // hg

    def logits(g, c, mx):
        k0 = pl.multiple_of(c * sc, sc)
        s = lax.dot_general(qs_ref[g * rows:(g + 1) * rows, :], k_ref[pl.ds(k0, sc), :], NT_DIMS,
                            preferred_element_type=F32)
        s = (s.reshape(hg, tq, sc) + bias_ref[c][None]).reshape(rows, sc)
        s_ref[g % 2, c] = s
        for j in range(sc // LANES):
            mx = jnp.maximum(mx, s[:, j * LANES:(j + 1) * LANES])
        return mx

    def weighted(g, c, m, acc):
        k0 = pl.multiple_of(c * sc, sc)
        p = jnp.exp2(s_ref[g % 2, c] - m)
        return acc + _dot(p.astype(BF16), va_ref[pl.ds(k0, sc), :])

    def finish(g, acc):
        os_ref[g * rows:(g + 1) * rows, :] = acc[:, :hd] / acc[:, hd:hd + 1]

    neg = jnp.full((rows, LANES), -jnp.inf, F32)
    zero = jnp.zeros((rows, 2 * LANES), F32)
    mx = lax.fori_loop(0, nkc, lambda c, mx: logits(0, c, mx), neg)
    for g in range(1, groups):
        m = jnp.max(mx, axis=-1, keepdims=True)
        mx, acc = lax.fori_loop(
            0, nkc, lambda c, st, g=g, m=m: (logits(g, c, st[0]), weighted(g - 1, c, m, st[1])), (neg, zero))
        finish(g - 1, acc)
    m = jnp.max(mx, axis=-1, keepdims=True)
    acc = lax.fori_loop(0, nkc, lambda c, acc, m=m: weighted(groups - 1, c, m, acc), zero)
    finish(groups - 1, acc)

    for h in range(DSA_HEADS):
        sl = slice(h * hd, (h + 1) * hd)
        o_ref[:, sl] = (os_ref[h * tq:(h + 1) * tq, :] * _silu(z_ref[:, sl])).astype(BF16)


def _dsa_core(q, qi, wi, z, k, va, ki, batch, tq=256, sc=512, hg=2):
    n = q.shape[0]
    seq = n // batch
    tq, sc = min(tq, seq), min(sc, seq)
    nq = seq // tq
    topk = min(TOPK_MAX, seq // 4)
    blk = lambda b, i: (b * nq + i, 0)
    full = lambda b, i: (b, 0)
    return pl.pallas_call(
        functools.partial(_dsa_core_kernel, topk, tq, sc, hg),
        grid=(batch, nq),
        in_specs=[pl.BlockSpec((tq, D_INNER), blk), pl.BlockSpec((tq, IDX_HEADS * IDX_DIM), blk),
                  pl.BlockSpec((tq, LANES), blk), pl.BlockSpec((tq, D_INNER), blk),
                  pl.BlockSpec((seq, LANES), full), pl.BlockSpec((seq, 2 * LANES), full),
                  pl.BlockSpec((seq, LANES), full)],
        out_specs=pl.BlockSpec((tq, D_INNER), blk),
        out_shape=jax.ShapeDtypeStruct((n, D_INNER), BF16),
        scratch_shapes=[pltpu.VMEM((seq // sc, sc, tq), jnp.int32),
                        pltpu.VMEM((seq // sc, tq, sc), F32),
                        pltpu.VMEM((DSA_HEADS * tq, DSA_HEAD_DIM), BF16),
                        pltpu.VMEM((DSA_HEADS * tq, DSA_HEAD_DIM), F32),
                        pltpu.VMEM((2, seq // sc, hg * tq, sc), F32)],
        compiler_params=_params("parallel", "arbitrary"),
        name="dsa_core",
    )(q, qi, wi, z, k, va, ki)


def _dsa_mixer(x, pos, w_in, batch):
    w = w_in.astype(BF16)
    hd = DSA_HEAD_DIM
    o1 = D_INNER
    o3 = o1 + 2 * hd
    o4 = o3 + D_INNER
    o5 = o4 + IDX_HEADS * IDX_DIM
    o6 = o5 + IDX_DIM
    q, k, va, qi, ki, wi = _dsa_prep(x, pos, w[:, :o1], w[:, o1:o3], w[:, o4:o5], w[:, o5:o6], w[:, o6:])
    z = _mm(x, w[:, o3:o4], F32)
    return _dsa_core(q, qi, wi, z, k, va, ki, batch)


CONV_HALO = 32


def _conv_core_kernel(a_ref, gate_ref, z_ref, w_ref, b_ref, lng_ref, lnb_ref, o_ref, buf_ref):
    t = a_ref.shape[0]

    @pl.when(pl.program_id(1) == 0)
    def _():
        buf_ref[0:CONV_HALO, :] = jnp.zeros((CONV_HALO, buf_ref.shape[1]), F32)

    buf_ref[CONV_HALO:CONV_HALO + t, :] = a_ref[...] * jax.nn.sigmoid(gate_ref[...])
    first = CONV_HALO - (CONV_WIDTH - 1)
    acc = jnp.zeros(a_ref.shape, F32)
    for j in range(CONV_WIDTH):
        acc = acc + buf_ref[first + j:first + j + t, :] * w_ref[j:j + 1, :]
    u = acc + b_ref[...]
    mu = jnp.mean(u, axis=-1, keepdims=True)
    d = u - mu
    var = jnp.mean(d * d, axis=-1, keepdims=True)
    u = _silu(d * lax.rsqrt(var + NORM_EPS) * lng_ref[...] + lnb_ref[...])
    o_ref[...] = (u * _silu(z_ref[...])).astype(BF16)
    buf_ref[0:CONV_HALO, :] = buf_ref[t:t + CONV_HALO, :]


def _conv_core(a, gate, z, dw_w, dw_b, ln_g, ln_b, batch, t=256):
    n, c = a.shape
    seq = n // batch
    t = min(t, seq)
    nt = seq // t
    blk = lambda b, i: (b * nt + i, 0)
    whole = lambda b, i: (0, 0)
    w_p = jnp.zeros((CONV_HALO, c), F32).at[:CONV_WIDTH].set(dw_w)
    vec = pl.BlockSpec((1, c), whole)
    return pl.pallas_call(
        _conv_core_kernel,
        grid=(batch, nt),
        in_specs=[pl.BlockSpec((t, c), blk)] * 3 + [pl.BlockSpec(w_p.shape, whole), vec, vec, vec],
        out_specs=pl.BlockSpec((t, c), blk),
        out_shape=jax.ShapeDtypeStruct((n, c), BF16),
        scratch_shapes=[pltpu.VMEM((t + CONV_HALO, c), F32)],
        compiler_params=_params("parallel", "arbitrary"),
        name="conv_core",
    )(a, gate, z, w_p, dw_b.reshape(1, c), ln_g.reshape(1, c), ln_b.reshape(1, c))


def _conv_mixer(x, w_in, dw_w, dw_b, ln_g, ln_b, batch):
    w = w_in.astype(BF16)
    a = _mm(x, w[:, :D_INNER], F32)
    gate = _mm(x, w[:, D_INNER:2 * D_INNER], F32)
    z = _mm(x, w[:, 2 * D_INNER:], F32)
    return _conv_core(a, gate, z, dw_w, dw_b, ln_g, ln_b, batch)


def kernel(x, p, positions, gla_w_in, gla_w_a2, gla_b_a, gla_gn_g, gla_w_out, dsa_w_in, dsa_w_out,
           conv_w_in, conv_dw_w, conv_dw_b, conv_ln_g, conv_ln_b, conv_w_out, ln_g, ln_b, ple_w,
           ple_gate_w):
    batch, seq, d = x.shape
    depth = p.shape[0]
    n = batch * seq
    alpha = (2 * depth) ** 0.25
    xf = x.reshape(n, d)
    pos = positions.astype(F32).reshape(n, 1)
    for i in range(depth):
        kind, j = i % N_MIXERS, i // N_MIXERS
        if kind == 0:
            u = _gla_mixer(xf, gla_w_in[j], gla_w_a2[j], gla_b_a[j], gla_gn_g[j], batch)
            w_out = gla_w_out[j]
        elif kind == 1:
            u = _dsa_mixer(xf, pos, dsa_w_in[j], batch)
            w_out = dsa_w_out[j]
        else:
            u = _conv_mixer(xf, conv_w_in[j], conv_dw_w[j], conv_dw_b[j], conv_ln_g[j], conv_ln_b[j], batch)
            w_out = conv_w_out[j]
        xf = _post(u, xf, p[i].reshape(n, -1), w_out, ln_g[i], ln_b[i], ple_w[i], ple_gate_w[i], alpha)
    return xf.reshape(batch, seq, d)
```

```python
import functools

import jax
import jax.numpy as jnp
from jax import lax
from jax.experimental import pallas as pl
from jax.experimental.pallas import tpu as pltpu

F32 = jnp.float32
BF16 = jnp.bfloat16

D_MODEL = 1024
N_MIXERS = 3
D_INNER = 2 * D_MODEL
GLA_HEADS = 4
GLA_DK = D_MODEL // GLA_HEADS
GLA_DV = D_INNER // GLA_HEADS
GLA_RANK = 16
GLA_GATE_NORM = 16.0
GLA_CHUNK = 64
DSA_HEADS = 16
DSA_HEAD_DIM = D_INNER // DSA_HEADS
IDX_HEADS = 8
IDX_DIM = 64
TOPK_MAX = 256
ATTN_SCALE = DSA_HEAD_DIM ** -0.5
INDEX_SCALE = (IDX_HEADS ** -0.5) * (IDX_DIM ** -0.5)
CONV_WIDTH = 31
ROPE_THETA = 500000.0
DSA_ROT_DIM = DSA_HEAD_DIM // 4
IDX_ROT_DIM = IDX_DIM // 4
PLE_DIM = 256
NORM_EPS = 1e-5

LANES = 128
LOG2E = 1.4426950408889634
INT_MIN = -(2 ** 31)
VMEM_LIMIT = 56 * 1024 * 1024

NT_DIMS = (((1,), (1,)), ((), ()))
TN_DIMS = (((0,), (0,)), ((), ()))


def _params(*sem):
    return pltpu.CompilerParams(dimension_semantics=sem, vmem_limit_bytes=VMEM_LIMIT)


def _dot(a, b):
    return jnp.dot(a, b, preferred_element_type=F32)


def _silu(x):
    return x * jax.nn.sigmoid(x)


def _mm_kernel(x_ref, w_ref, o_ref):
    o_ref[...] = _dot(x_ref[...].astype(BF16), w_ref[...]).astype(o_ref.dtype)


def _mm(x, w, out_dtype, tm=1024, tn=1024):
    n, k = x.shape
    c = w.shape[1]
    tm, tn = min(tm, n), min(tn, c)
    return pl.pallas_call(
        _mm_kernel,
        grid=(n // tm, c // tn),
        in_specs=[pl.BlockSpec((tm, k), lambda i, j: (i, 0)),
                  pl.BlockSpec((k, tn), lambda i, j: (0, j))],
        out_specs=pl.BlockSpec((tm, tn), lambda i, j: (i, j)),
        out_shape=jax.ShapeDtypeStruct((n, c), out_dtype),
        compiler_params=_params("parallel", "parallel"),
        name="proj",
    )(x, w)


def _post_kernel(alpha, u_ref, x_ref, p_ref, wout_ref, lng_ref, lnb_ref, plew_ref, gatew_ref, o_ref):
    y = _dot(u_ref[...], wout_ref[...])
    r = alpha * x_ref[...] + y
    mu = jnp.mean(r, axis=-1, keepdims=True)
    d = r - mu
    var = jnp.mean(d * d, axis=-1, keepdims=True)
    x1 = d * lax.rsqrt(var + NORM_EPS) * lng_ref[...] + lnb_ref[...]
    ple = _dot(p_ref[...].astype(BF16), plew_ref[...])
    gate = _dot(x1.astype(BF16), gatew_ref[...])
    o_ref[...] = x1 + ple * jax.nn.sigmoid(gate)


def _post(u, x, p, w_out, ln_g, ln_b, ple_w, gate_w, alpha, tm=512):
    n, d = x.shape
    tm = min(tm, n)
    row = lambda i: (i, 0)
    whole = lambda i: (0, 0)
    return pl.pallas_call(
        functools.partial(_post_kernel, alpha),
        grid=(n // tm,),
        in_specs=[pl.BlockSpec((tm, u.shape[1]), row),
                  pl.BlockSpec((tm, d), row),
                  pl.BlockSpec((tm, p.shape[1]), row),
                  pl.BlockSpec(w_out.shape, whole),
                  pl.BlockSpec((1, d), whole),
                  pl.BlockSpec((1, d), whole),
                  pl.BlockSpec(ple_w.shape, whole),
                  pl.BlockSpec(gate_w.shape, whole)],
        out_specs=pl.BlockSpec((tm, d), row),
        out_shape=jax.ShapeDtypeStruct((n, d), F32),
        compiler_params=_params("parallel"),
        name="post",
    )(u, x, p, w_out.astype(BF16), ln_g.reshape(1, d), ln_b.reshape(1, d),
      ple_w.astype(BF16), gate_w.astype(BF16))


def _split3_bf16(x):
    hi = x.astype(BF16)
    r = x - hi.astype(F32)
    mid = r.astype(BF16)
    lo = (r - mid.astype(F32)).astype(BF16)
    return hi, mid, lo


def _gla_gate_kernel(x_ref, wq_ref, wk_ref, wa_ref, wa2_ref, ba_ref,
                     qt_ref, kt_ref, ks_ref, dec_ref):
    xb = x_ref[...].astype(BF16)
    q = _dot(xb, wq_ref[...])
    k = _dot(xb, wk_ref[...])
    a = _dot(xb, wa_ref[...])
    pre = _dot(a.astype(BF16), wa2_ref[...]) + ba_ref[...]
    g = (jnp.minimum(pre, 0.0) - jnp.log(1.0 + jnp.exp(-jnp.abs(pre)))) / GLA_GATE_NORM
    ch = GLA_CHUNK
    ri = lax.broadcasted_iota(jnp.int32, (ch, ch), 0)
    ci = lax.broadcasted_iota(jnp.int32, (ch, ch), 1)
    tri = jnp.where(ri >= ci, 1.0, 0.0).astype(BF16)
    decs = []
    for c in range(x_ref.shape[0] // ch):
        sl = slice(c * ch, (c + 1) * ch)
        hi, mid, lo = _split3_bf16(g[sl])
        cum = _dot(tri, hi) + _dot(tri, mid) + _dot(tri, lo)
        last = cum[ch - 1:ch]
        qt_ref[sl, :] = (q[sl] * (GLA_DK ** -0.5) * jnp.exp(cum)).astype(BF16)
        kt_ref[sl, :] = (k[sl] * jnp.exp(-cum)).astype(BF16)
        ks_ref[sl, :] = (k[sl] * jnp.exp(last - cum)).astype(BF16)
        decs.append(jnp.exp(last))
    dec_ref[...] = jnp.concatenate(decs, axis=0)


def _gla_gate(x, wq, wk, wa, wa2, ba, tm=512):
    n, d = x.shape
    qk = wq.shape[1]
    row = lambda i: (i, 0)
    whole = lambda i: (0, 0)
    wa_p = jnp.zeros((d, LANES), BF16).at[:, :GLA_RANK].set(wa)
    wa2_p = jnp.zeros((LANES, qk), BF16).at[:GLA_RANK].set(wa2)
    act = jax.ShapeDtypeStruct((n, qk), BF16)
    return pl.pallas_call(
        _gla_gate_kernel,
        grid=(n // tm,),
        in_specs=[pl.BlockSpec((tm, d), row),
                  pl.BlockSpec(wq.shape, whole), pl.BlockSpec(wk.shape, whole),
                  pl.BlockSpec(wa_p.shape, whole), pl.BlockSpec(wa2_p.shape, whole),
                  pl.BlockSpec((1, qk), whole)],
        out_specs=[pl.BlockSpec((tm, qk), row)] * 3 + [pl.BlockSpec((tm // GLA_CHUNK, qk), row)],
        out_shape=[act, act, act, jax.ShapeDtypeStruct((n // GLA_CHUNK, qk), F32)],
        compiler_params=_params("parallel"),
        name="gla_gate",
    )(x, wq, wk, wa_p, wa2_p, ba.reshape(1, qk))


def _gla_core_kernel(qt_ref, kt_ref, ks_ref, v_ref, dec_ref, z_ref, gn_ref, o_ref, s_ref):
    @pl.when(pl.program_id(2) == 0)
    def _():
        s_ref[...] = jnp.zeros_like(s_ref)

    ch = GLA_CHUNK
    dec_t = dec_ref[...].T
    ri = lax.broadcasted_iota(jnp.int32, (ch, ch), 0)
    ci = lax.broadcasted_iota(jnp.int32, (ch, ch), 1)
    tril = ri >= ci
    for c in range(qt_ref.shape[0] // ch):
        sl = slice(c * ch, (c + 1) * ch)
        qt, kt, ks, v = qt_ref[sl, :], kt_ref[sl, :], ks_ref[sl, :], v_ref[sl, :]
        a = jnp.where(tril, lax.dot_general(qt, kt, NT_DIMS, preferred_element_type=F32), 0.0)
        s = s_ref[...]
        o = _dot(a.astype(BF16), v) + _dot(qt, s.astype(BF16))
        s_ref[...] = s * dec_t[:, c:c + 1] + lax.dot_general(ks, v, TN_DIMS, preferred_element_type=F32)
        o = o * lax.rsqrt(jnp.mean(o * o, axis=-1, keepdims=True) + NORM_EPS)
        o = o * gn_ref[...]
        o_ref[sl, :] = (o * _silu(z_ref[sl, :])).astype(BF16)


def _gla_core(qt, kt, ks, v, dec, z, gn_g, batch, tc=512):
    n = qt.shape[0]
    seq = n // batch
    tc = min(tc, seq)
    nt = seq // tc
    tok = lambda b, h, i: (b * nt + i, h)
    return pl.pallas_call(
        _gla_core_kernel,
        grid=(batch, GLA_HEADS, nt),
        in_specs=[pl.BlockSpec((tc, GLA_DK), tok)] * 3
                 + [pl.BlockSpec((tc, GLA_DV), tok),
                    pl.BlockSpec((tc // GLA_CHUNK, GLA_DK), tok),
                    pl.BlockSpec((tc, GLA_DV), tok),
                    pl.BlockSpec((1, GLA_DV), lambda b, h, i: (0, h))],
        out_specs=pl.BlockSpec((tc, GLA_DV), tok),
        out_shape=jax.ShapeDtypeStruct((n, D_INNER), BF16),
        scratch_shapes=[pltpu.VMEM((GLA_DK, GLA_DV), F32)],
        compiler_params=_params("parallel", "parallel", "arbitrary"),
        name="gla_core",
    )(qt, kt, ks, v, dec, z, gn_g.reshape(1, D_INNER))


def _gla_mixer(x, w_in, w_a2, b_a, gn_g, batch):
    qk = GLA_HEADS * GLA_DK
    w = w_in.astype(BF16)
    o_v, o_z, o_a = 2 * qk, 2 * qk + D_INNER, 2 * qk + 2 * D_INNER
    qt, kt, ks, dec = _gla_gate(x, w[:, :qk], w[:, qk:o_v], w[:, o_a:], w_a2.astype(BF16), b_a)
    v = _mm(x, w[:, o_v:o_z], BF16)
    z = _mm(x, w[:, o_z:o_a], F32)
    return _gla_core(qt, kt, ks, v, dec, z, gn_g, batch)


def _rope(x, cos, sin_lo, sin_hi, half):
    return (x * cos + pltpu.roll(x, half, 1) * sin_hi
            + pltpu.roll(x, LANES - half, 1) * sin_lo)


def _rope_tables(pos, invf, half, period):
    ang = pos * invf
    cos, sin = jnp.cos(ang), jnp.sin(ang)
    lane = lax.broadcasted_iota(jnp.int32, ang.shape, 1) % period
    sin_lo = jnp.where(lane < half, -sin, 0.0)
    sin_hi = jnp.where((lane >= half) & (lane < 2 * half), sin, 0.0)
    return cos, sin_lo, sin_hi


def _dsa_prep_kernel(x_ref, pos_ref, invh_ref, invi_ref, wq_ref, wkv_ref, wqi_ref, wki_ref, wwi_ref,
                     q_ref, k_ref, v_ref, qi_ref, ki_ref, wi_ref):
    xb = x_ref[...].astype(BF16)
    pos = pos_ref[...]
    hd = DSA_HEAD_DIM
    cos_h, slo_h, shi_h = _rope_tables(pos, invh_ref[...], DSA_ROT_DIM // 2, hd)
    cos_i, slo_i, shi_i = _rope_tables(pos, invi_ref[...], IDX_ROT_DIM // 2, IDX_DIM)
    q = _dot(xb, wq_ref[...])
    for h in range(DSA_HEADS):
        sl = slice(h * hd, (h + 1) * hd)
        q_ref[:, sl] = (_rope(q[:, sl], cos_h, slo_h, shi_h, DSA_ROT_DIM // 2)
                        * (ATTN_SCALE * LOG2E)).astype(BF16)
    kv = _dot(xb, wkv_ref[...])
    k_ref[...] = _rope(kv[:, :hd], cos_h, slo_h, shi_h, DSA_ROT_DIM // 2).astype(BF16)
    ones_col = jnp.where(lax.broadcasted_iota(jnp.int32, (x_ref.shape[0], LANES), 1) == 0, 1.0, 0.0)
    v_ref[:, :hd] = kv[:, hd:].astype(BF16)
    v_ref[:, hd:] = ones_col.astype(BF16)
    qi = _dot(xb, wqi_ref[...])
    for j in range(qi.shape[1] // LANES):
        sl = slice(j * LANES, (j + 1) * LANES)
        qi_ref[:, sl] = _rope(qi[:, sl], cos_i, slo_i, shi_i, IDX_ROT_DIM // 2).astype(BF16)
    ki = _dot(xb, wki_ref[...])
    ki_ref[...] = _rope(ki, cos_i, slo_i, shi_i, IDX_ROT_DIM // 2).astype(BF16)
    wi_ref[...] = _dot(xb, wwi_ref[...])


def _lane_invfreq(rot_dim, period):
    inv = ROPE_THETA ** (-jnp.arange(0, rot_dim, 2, dtype=F32) / rot_dim)
    head = jnp.concatenate([inv, inv, jnp.zeros((period - rot_dim,), F32)])
    return jnp.tile(head, LANES // period).reshape(1, LANES)


def _pad_cols(w, width):
    return jnp.zeros((w.shape[0], width), w.dtype).at[:, :w.shape[1]].set(w)


def _dsa_prep(x, pos, wq, wkv, wqi, wki, wwi, tm=512):
    n, d = x.shape
    tm = min(tm, n)
    row = lambda i: (i, 0)
    whole = lambda i: (0, 0)
    wki_p, wwi_p = _pad_cols(wki, LANES), _pad_cols(wwi, LANES)
    ws = [wq, wkv, wqi, wki_p, wwi_p]
    return pl.pallas_call(
        _dsa_prep_kernel,
        grid=(n // tm,),
        in_specs=[pl.BlockSpec((tm, d), row), pl.BlockSpec((tm, 1), row),
                  pl.BlockSpec((1, LANES), whole), pl.BlockSpec((1, LANES), whole)]
                 + [pl.BlockSpec(w.shape, whole) for w in ws],
        out_specs=[pl.BlockSpec((tm, D_INNER), row), pl.BlockSpec((tm, LANES), row),
                   pl.BlockSpec((tm, 2 * LANES), row), pl.BlockSpec((tm, IDX_HEADS * IDX_DIM), row),
                   pl.BlockSpec((tm, LANES), row), pl.BlockSpec((tm, LANES), row)],
        out_shape=[jax.ShapeDtypeStruct((n, D_INNER), BF16), jax.ShapeDtypeStruct((n, LANES), BF16),
                   jax.ShapeDtypeStruct((n, 2 * LANES), BF16),
                   jax.ShapeDtypeStruct((n, IDX_HEADS * IDX_DIM), BF16),
                   jax.ShapeDtypeStruct((n, LANES), BF16), jax.ShapeDtypeStruct((n, LANES), F32)],
        compiler_params=_params("parallel"),
        name="dsa_prep",
    )(x, pos, _lane_invfreq(DSA_ROT_DIM, DSA_HEAD_DIM), _lane_invfreq(IDX_ROT_DIM, IDX_DIM), *ws)


COUNT_ROWS = 64
MASKED = -1e30
QSUB = 128


def _sortable_key(score):
    b = pltpu.bitcast(score, jnp.int32)
    key = b ^ ((b >> 31) & 0x7FFFFFFF)
    return jnp.where(score == 0.0, 0, key)


def _dsa_core_kernel(topk, tq, sc, hg,
                     q_ref, qi_ref, wi_ref, z_ref, k_ref, va_ref, ki_ref, o_ref,
                     key_ref, hi_ref, lo_ref, kx_ref, qx_ref, os_ref, s_ref, mx_ref, m_ref, acc_ref):
    qb = pl.program_id(1)
    t0 = qb * tq
    nkc = (t0 + tq + sc - 1) // sc
    hd = DSA_HEAD_DIM
    nsub = tq // QSUB
    groups = DSA_HEADS // hg
    rows = hg * QSUB

    @pl.when(qb == 0)
    def _():
        for a in range(nsub):
            kx_ref[a, :, :hd] = k_ref[...]
        ri = lax.broadcasted_iota(jnp.int32, (QSUB, LANES), 0)
        ci = lax.broadcasted_iota(jnp.int32, (QSUB, LANES), 1)
        eye = jnp.where(ri == ci, 1.0, 0.0).astype(BF16)
        for j in range(nsub * DSA_HEADS):
            qx_ref[j * QSUB:(j + 1) * QSUB, hd:] = eye

    qi_s = jnp.concatenate([qi_ref[:, h * IDX_DIM:(h + 1) * IDX_DIM] for h in range(IDX_HEADS)], axis=0)
    wi_t = wi_ref[...].T
    wi_row = jnp.concatenate([wi_t[h:h + 1, :] for h in range(IDX_HEADS)], axis=1)
    k_pos = lax.broadcasted_iota(jnp.int32, (sc, tq), 0)
    t_pos = t0 + lax.broadcasted_iota(jnp.int32, (sc, tq), 1)

    def score_body(c, carry):
        ki_c = ki_ref[pl.ds(pl.multiple_of(c * sc, sc), sc), :][:, :IDX_DIM]
        rel = jnp.maximum(lax.dot_general(ki_c, qi_s, NT_DIMS, preferred_element_type=F32), 0.0) * wi_row
        score = rel[:, :tq]
        for h in range(1, IDX_HEADS):
            score = score + rel[:, h * tq:(h + 1) * tq]
        score = score * INDEX_SCALE
        key = jnp.where(c * sc + k_pos <= t_pos, _sortable_key(score), INT_MIN)
        key_ref[c] = key
        hi_ref[c] = (key >> 16).astype(jnp.int16)
        return carry

    lax.fori_loop(0, nkc, score_body, 0)

    def count16(ref, pred):
        def body(c, acc):
            m = jnp.where(pred(ref[c]), jnp.bfloat16(1), jnp.bfloat16(0))
            for j in range(sc // COUNT_ROWS):
                acc = acc + m[j * COUNT_ROWS:(j + 1) * COUNT_ROWS]
            return acc
        acc = lax.fori_loop(0, nkc, body, jnp.zeros((COUNT_ROWS, tq), BF16))
        return jnp.sum(acc.astype(F32), axis=0, keepdims=True)

    def search16(ref, need):
        def body(i, r):
            cand = r | (jnp.int32(1) << (15 - i))
            cand16 = (cand - 32768).astype(jnp.int16)
            return jnp.where(count16(ref, lambda v: v >= cand16) >= need, cand, r)
        return lax.fori_loop(0, 16, body, jnp.zeros((1, tq), jnp.int32)) - 32768

    t_hi = search16(hi_ref, topk)
    t_hi16 = t_hi.astype(jnp.int16)
    n_hi_gt = count16(hi_ref, lambda v: v > t_hi16)

    def lo_body(c, carry):
        key = key_ref[c]
        lo_ref[c] = jnp.where((key >> 16) == t_hi, (key & 0xFFFF) - 32768, -32768).astype(jnp.int16)
        return carry

    lax.fori_loop(0, nkc, lo_body, 0)
    t_lo = search16(lo_ref, topk - n_hi_gt)
    thr = (t_hi << 16) | (t_lo + 32768)

    def count(pred_fn):
        def body(c, acc):
            m = jnp.where(pred_fn(key_ref[c], c), 1.0, 0.0)
            return acc + jnp.sum(m.reshape(sc // COUNT_ROWS, COUNT_ROWS, tq), axis=0)
        acc = lax.fori_loop(0, nkc, body, jnp.zeros((COUNT_ROWS, tq), F32))
        return jnp.sum(acc, axis=0, keepdims=True)

    n_gt = count(lambda key, c: key > thr)
    n_ge = count(lambda key, c: key >= thr)
    need = topk - n_gt
    has_thr = thr != INT_MIN
    excess = jnp.max(jnp.where(has_thr, n_ge - n_gt - need, 0.0))

    def tie_limit():
        def body(i, r):
            cand = r | (jnp.int32(1) << (15 - i))
            n = count(lambda key, c: (key == thr) & (c * sc + k_pos < cand))
            return jnp.where(n < need, cand, r)
        return lax.fori_loop(0, 16, body, jnp.zeros((1, tq), jnp.int32))

    limit = lax.cond(excess > 0.0, tie_limit, lambda: jnp.full((1, tq), 2 ** 30, jnp.int32))
    limit = jnp.where(has_thr, limit, -1)

    def mask_body(c, carry):
        key = key_ref[c]
        sel = (key > thr) | ((key == thr) & (c * sc + k_pos <= limit))
        mask = jnp.where(sel, 0.0, MASKED).astype(BF16)
        k0 = pl.multiple_of(c * sc, sc)
        for a in range(nsub):
            kx_ref[a, pl.ds(k0, sc), hd:] = mask[:, a * QSUB:(a + 1) * QSUB]
        return carry

    lax.fori_loop(0, nkc, mask_body, 0)

    for a in range(nsub):
        for h in range(DSA_HEADS):
            j = a * DSA_HEADS + h
            qx_ref[j * QSUB:(j + 1) * QSUB, :hd] = q_ref[a * QSUB:(a + 1) * QSUB, h * hd:(h + 1) * hd]

    def logits(u, c):
        k0 = pl.multiple_of(c * sc, sc)
        s = lax.dot_general(qx_ref[u * rows:(u + 1) * rows, :], kx_ref[u // groups, pl.ds(k0, sc), :],
                            NT_DIMS, preferred_element_type=F32)
        s_ref[c] = s
        mx = mx_ref[...]
        for j in range(sc // LANES):
            mx = jnp.maximum(mx, s[:, j * LANES:(j + 1) * LANES])
        mx_ref[...] = mx

    def weighted(c):
        k0 = pl.multiple_of(c * sc, sc)
        m = m_ref[...]
        p = jnp.concatenate([jnp.exp2(s_ref[c, :, j * LANES:(j + 1) * LANES] - m)
                             for j in range(sc // LANES)], axis=1)
        acc_ref[...] += _dot(p.astype(BF16), va_ref[pl.ds(k0, sc), :])

    def start_unit():
        m_ref[...] = jnp.broadcast_to(jnp.max(mx_ref[...], axis=-1, keepdims=True), m_ref.shape)
        mx_ref[...] = jnp.full(mx_ref.shape, -jnp.inf, F32)
        acc_ref[...] = jnp.zeros(acc_ref.shape, F32)

    def finish(u):
        acc = acc_ref[...]
        os_ref[u * rows:(u + 1) * rows, :] = acc[:, :hd] / acc[:, hd:hd + 1]

    def run(body):
        lax.fori_loop(0, nkc, lambda c, carry: (body(c), carry)[1], 0)

    units = nsub * groups
    mx_ref[...] = jnp.full(mx_ref.shape, -jnp.inf, F32)
    run(lambda c: logits(0, c))
    for u in range(1, units):
        start_unit()
        run(lambda c, u=u: (weighted(c), logits(u, c)))
        finish(u - 1)
    start_unit()
    run(weighted)
    finish(units - 1)

    for a in range(nsub):
        for h in range(DSA_HEADS):
            j = a * DSA_HEADS + h
            qr, sl = slice(a * QSUB, (a + 1) * QSUB), slice(h * hd, (h + 1) * hd)
            o_ref[qr, sl] = (os_ref[j * QSUB:(j + 1) * QSUB, :] * _silu(z_ref[qr, sl])).astype(BF16)


def _dsa_core(q, qi, wi, z, k, va, ki, batch, tq=256, sc=512, hg=8):
    n = q.shape[0]
    seq = n // batch
    tq, sc = min(tq, seq), min(sc, seq)
    nq = seq // tq
    topk = min(TOPK_MAX, seq // 4)
    assert seq // COUNT_ROWS <= 256 and seq <= 2 ** 16 and tq % QSUB == 0
    stacked = (tq // QSUB) * DSA_HEADS * QSUB
    blk = lambda b, i: (b * nq + i, 0)
    full = lambda b, i: (b, 0)
    return pl.pallas_call(
        functools.partial(_dsa_core_kernel, topk, tq, sc, hg),
        grid=(batch, nq),
        in_specs=[pl.BlockSpec((tq, D_INNER), blk), pl.BlockSpec((tq, IDX_HEADS * IDX_DIM), blk),
                  pl.BlockSpec((tq, LANES), blk), pl.BlockSpec((tq, D_INNER), blk),
                  pl.BlockSpec((seq, LANES), full), pl.BlockSpec((seq, 2 * LANES), full),
                  pl.BlockSpec((seq, LANES), full)],
        out_specs=pl.BlockSpec((tq, D_INNER), blk),
        out_shape=jax.ShapeDtypeStruct((n, D_INNER), BF16),
        scratch_shapes=[pltpu.VMEM((seq // sc, sc, tq), jnp.int32),
                        pltpu.VMEM((seq // sc, sc, tq), jnp.int16),
                        pltpu.VMEM((seq // sc, sc, tq), jnp.int16),
                        pltpu.VMEM((tq // QSUB, seq, 2 * LANES), BF16),
                        pltpu.VMEM((stacked, 2 * LANES), BF16),
                        pltpu.VMEM((stacked, DSA_HEAD_DIM), F32),
                        pltpu.VMEM((seq // sc, hg * QSUB, sc), F32),
                        pltpu.VMEM((hg * QSUB, LANES), F32),
                        pltpu.VMEM((hg * QSUB, LANES), F32),
                        pltpu.VMEM((hg * QSUB, 2 * LANES), F32)],
        compiler_params=_params("parallel", "arbitrary"),
        name="dsa_core",
    )(q, qi, wi, z, k, va, ki)


def _dsa_mixer(x, pos, w_in, batch):
    w = w_in.astype(BF16)
    hd = DSA_HEAD_DIM
    o1 = D_INNER
    o3 = o1 + 2 * hd
    o4 = o3 + D_INNER
    o5 = o4 + IDX_HEADS * IDX_DIM
    o6 = o5 + IDX_DIM
    q, k, va, qi, ki, wi = _dsa_prep(x, pos, w[:, :o1], w[:, o1:o3], w[:, o4:o5], w[:, o5:o6], w[:, o6:])
    z = _mm(x, w[:, o3:o4], F32)
    return _dsa_core(q, qi, wi, z, k, va, ki, batch)


CONV_HALO = 32
CONV_ROWS = 64
CONV_COLS = 256
SUBLANES = 8
NORM_ROWS = 128


def _conv_core_kernel(a_ref, gate_ref, z_ref, w_ref, b_ref, lng_ref, lnb_ref, o_ref,
                      buf_ref, sh_ref, conv_ref):
    t, ch = a_ref.shape

    @pl.when(pl.program_id(1) == 0)
    def _():
        buf_ref[0:CONV_HALO, :] = jnp.zeros((CONV_HALO, ch), F32)

    buf_ref[CONV_HALO:CONV_HALO + t, :] = a_ref[...] * jax.nn.sigmoid(gate_ref[...])
    first = CONV_HALO - (CONV_WIDTH - 1)
    span = t + CONV_HALO - SUBLANES
    for cb in range(ch // CONV_COLS):
        cols = slice(cb * CONV_COLS, (cb + 1) * CONV_COLS)
        for s in range(1, SUBLANES):
            sh_ref[s - 1] = buf_ref[s:s + span, cols]

        def tile(i, carry, cols=cols):
            r0 = pl.multiple_of(i * CONV_ROWS, CONV_ROWS)
            acc = jnp.zeros((CONV_ROWS, CONV_COLS), F32)
            for j in range(CONV_WIDTH):
                s = (first + j) % SUBLANES
                base = r0 + (first + j - s)
                if s == 0:
                    src = buf_ref[pl.ds(base, CONV_ROWS), cols]
                else:
                    src = sh_ref[s - 1, pl.ds(base, CONV_ROWS), :]
                acc = acc + src * w_ref[j:j + 1, cols]
            conv_ref[pl.ds(r0, CONV_ROWS), cols] = acc + b_ref[:, cols]
            return carry

        lax.fori_loop(0, t // CONV_ROWS, tile, 0)

    def norm_tile(i, carry):
        rows = pl.ds(pl.multiple_of(i * NORM_ROWS, NORM_ROWS), NORM_ROWS)
        u = conv_ref[rows, :]
        mu = jnp.mean(u, axis=-1, keepdims=True)
        d = u - mu
        var = jnp.mean(d * d, axis=-1, keepdims=True)
        u = _silu(d * lax.rsqrt(var + NORM_EPS) * lng_ref[...] + lnb_ref[...])
        o_ref[rows, :] = (u * _silu(z_ref[rows, :])).astype(BF16)
        return carry

    lax.fori_loop(0, t // NORM_ROWS, norm_tile, 0)
    buf_ref[0:CONV_HALO, :] = buf_ref[t:t + CONV_HALO, :]


def _conv_core(a, gate, z, dw_w, dw_b, ln_g, ln_b, batch, t=256):
    n, c = a.shape
    seq = n // batch
    t = min(t, seq)
    nt = seq // t
    blk = lambda b, i: (b * nt + i, 0)
    whole = lambda b, i: (0, 0)
    w_p = jnp.zeros((CONV_HALO, c), F32).at[:CONV_WIDTH].set(dw_w)
    vec = pl.BlockSpec((1, c), whole)
    return pl.pallas_call(
        _conv_core_kernel,
        grid=(batch, nt),
        in_specs=[pl.BlockSpec((t, c), blk)] * 3 + [pl.BlockSpec(w_p.shape, whole), vec, vec, vec],
        out_specs=pl.BlockSpec((t, c), blk),
        out_shape=jax.ShapeDtypeStruct((n, c), BF16),
        scratch_shapes=[pltpu.VMEM((t + CONV_HALO, c), F32),
                        pltpu.VMEM((SUBLANES - 1, t + CONV_HALO - SUBLANES, CONV_COLS), F32),
                        pltpu.VMEM((t, c), F32)],
        compiler_params=_params("parallel", "arbitrary"),
        name="conv_core",
    )(a, gate, z, w_p, dw_b.reshape(1, c), ln_g.reshape(1, c), ln_b.reshape(1, c))


def _conv_mixer(x, w_in, dw_w, dw_b, ln_g, ln_b, batch):
    w = w_in.astype(BF16)
    a = _mm(x, w[:, :D_INNER], F32)
    gate = _mm(x, w[:, D_INNER:2 * D_INNER], F32)
    z = _mm(x, w[:, 2 * D_INNER:], F32)
    return _conv_core(a, gate, z, dw_w, dw_b, ln_g, ln_b, batch)


def kernel(x, p, positions, gla_w_in, gla_w_a2, gla_b_a, gla_gn_g, gla_w_out, dsa_w_in, dsa_w_out,
           conv_w_in, conv_dw_w, conv_dw_b, conv_ln_g, conv_ln_b, conv_w_out, ln_g, ln_b, ple_w,
           ple_gate_w):
    batch, seq, d = x.shape
    depth = p.shape[0]
    n = batch * seq
    alpha = (2 * depth) ** 0.25
    xf = x.reshape(n, d)
    pos = positions.astype(F32).reshape(n, 1)
    for i in range(depth):
        kind, j = i % N_MIXERS, i // N_MIXERS
        if kind == 0:
            u = _gla_mixer(xf, gla_w_in[j], gla_w_a2[j], gla_b_a[j], gla_gn_g[j], batch)
            w_out = gla_w_out[j]
        elif kind == 1:
            u = _dsa_mixer(xf, pos, dsa_w_in[j], batch)
            w_out = dsa_w_out[j]
        else:
            u = _conv_mixer(xf, conv_w_in[j], conv_dw_w[j], conv_dw_b[j], conv_ln_g[j], conv_ln_b[j], batch)
            w_out = conv_w_out[j]
        xf = _post(u, xf, p[i].reshape(n, -1), w_out, ln_g[i], ln_b[i], ple_w[i], ple_gate_w[i], alpha)
    return xf.reshape(batch, seq, d)
```

```python
import functools

import jax
import jax.numpy as jnp
from jax import lax
from jax.experimental import pallas as pl
from jax.experimental.pallas import tpu as pltpu

F32 = jnp.float32
BF16 = jnp.bfloat16

D_MODEL = 1024
N_MIXERS = 3
D_INNER = 2 * D_MODEL
GLA_HEADS = 4
GLA_DK = D_MODEL // GLA_HEADS
GLA_DV = D_INNER // GLA_HEADS
GLA_RANK = 16
GLA_GATE_NORM = 16.0
GLA_CHUNK = 64
DSA_HEADS = 16
DSA_HEAD_DIM = D_INNER // DSA_HEADS
IDX_HEADS = 8
IDX_DIM = 64
TOPK_MAX = 256
ATTN_SCALE = DSA_HEAD_DIM ** -0.5
INDEX_SCALE = (IDX_HEADS ** -0.5) * (IDX_DIM ** -0.5)
CONV_WIDTH = 31
ROPE_THETA = 500000.0
DSA_ROT_DIM = DSA_HEAD_DIM // 4
IDX_ROT_DIM = IDX_DIM // 4
PLE_DIM = 256
NORM_EPS = 1e-5

LANES = 128
LOG2E = 1.4426950408889634
INT_MIN = -(2 ** 31)
VMEM_LIMIT = 56 * 1024 * 1024

NT_DIMS = (((1,), (1,)), ((), ()))
TN_DIMS = (((0,), (0,)), ((), ()))


def _params(*sem):
    return pltpu.CompilerParams(dimension_semantics=sem, vmem_limit_bytes=VMEM_LIMIT)


def _dot(a, b):
    return jnp.dot(a, b, preferred_element_type=F32)


def _silu(x):
    return x * jax.nn.sigmoid(x)


def _mm_kernel(x_ref, w_ref, o_ref):
    o_ref[...] = _dot(x_ref[...].astype(BF16), w_ref[...]).astype(o_ref.dtype)


def _mm(x, w, out_dtype, tm=1024, tn=1024):
    n, k = x.shape
    c = w.shape[1]
    tm, tn = min(tm, n), min(tn, c)
    return pl.pallas_call(
        _mm_kernel,
        grid=(n // tm, c // tn),
        in_specs=[pl.BlockSpec((tm, k), lambda i, j: (i, 0)),
                  pl.BlockSpec((k, tn), lambda i, j: (0, j))],
        out_specs=pl.BlockSpec((tm, tn), lambda i, j: (i, j)),
        out_shape=jax.ShapeDtypeStruct((n, c), out_dtype),
        compiler_params=_params("parallel", "parallel"),
        name="proj",
    )(x, w)


def _mm_glu_kernel(x_ref, wa_ref, wg_ref, o_ref):
    xb = x_ref[...].astype(BF16)
    o_ref[...] = _dot(xb, wa_ref[...]) * jax.nn.sigmoid(_dot(xb, wg_ref[...]))


def _mm_glu(x, wa, wg, tm=1024, tn=512):
    n, k = x.shape
    c = wa.shape[1]
    tm, tn = min(tm, n), min(tn, c)
    wspec = pl.BlockSpec((k, tn), lambda i, j: (0, j))
    return pl.pallas_call(
        _mm_glu_kernel,
        grid=(n // tm, c // tn),
        in_specs=[pl.BlockSpec((tm, k), lambda i, j: (i, 0)), wspec, wspec],
        out_specs=pl.BlockSpec((tm, tn), lambda i, j: (i, j)),
        out_shape=jax.ShapeDtypeStruct((n, c), F32),
        compiler_params=_params("parallel", "parallel"),
        name="proj_glu",
    )(x, wa, wg)


def _post_kernel(alpha, u_ref, x_ref, p_ref, wout_ref, lng_ref, lnb_ref, plew_ref, gatew_ref, o_ref, ob_ref):
    y = _dot(u_ref[...], wout_ref[...])
    r = alpha * x_ref[...] + y
    mu = jnp.mean(r, axis=-1, keepdims=True)
    d = r - mu
    var = jnp.mean(d * d, axis=-1, keepdims=True)
    x1 = d * lax.rsqrt(var + NORM_EPS) * lng_ref[...] + lnb_ref[...]
    ple = _dot(p_ref[...].astype(BF16), plew_ref[...])
    gate = _dot(x1.astype(BF16), gatew_ref[...])
    out = x1 + ple * jax.nn.sigmoid(gate)
    o_ref[...] = out
    ob_ref[...] = out.astype(BF16)


def _post(u, x, p, w_out, ln_g, ln_b, ple_w, gate_w, alpha, tm=512):
    n, d = x.shape
    tm = min(tm, n)
    row = lambda i: (i, 0)
    whole = lambda i: (0, 0)
    return pl.pallas_call(
        functools.partial(_post_kernel, alpha),
        grid=(n // tm,),
        in_specs=[pl.BlockSpec((tm, u.shape[1]), row),
                  pl.BlockSpec((tm, d), row),
                  pl.BlockSpec((tm, p.shape[1]), row),
                  pl.BlockSpec(w_out.shape, whole),
                  pl.BlockSpec((1, d), whole),
                  pl.BlockSpec((1, d), whole),
                  pl.BlockSpec(ple_w.shape, whole),
                  pl.BlockSpec(gate_w.shape, whole)],
        out_specs=[pl.BlockSpec((tm, d), row)] * 2,
        out_shape=[jax.ShapeDtypeStruct((n, d), F32), jax.ShapeDtypeStruct((n, d), BF16)],
        compiler_params=_params("parallel"),
        name="post",
    )(u, x, p, w_out.astype(BF16), ln_g.reshape(1, d), ln_b.reshape(1, d),
      ple_w.astype(BF16), gate_w.astype(BF16))


def _split3_bf16(x):
    hi = x.astype(BF16)
    r = x - hi.astype(F32)
    mid = r.astype(BF16)
    lo = (r - mid.astype(F32)).astype(BF16)
    return hi, mid, lo


def _gla_gate_kernel(x_ref, wq_ref, wk_ref, wa_ref, wa2_ref, ba_ref,
                     qt_ref, kt_ref, ks_ref, dec_ref):
    xb = x_ref[...].astype(BF16)
    q = _dot(xb, wq_ref[...])
    k = _dot(xb, wk_ref[...])
    a = _dot(xb, wa_ref[...])
    pre = _dot(a.astype(BF16), wa2_ref[...]) + ba_ref[...]
    g = (jnp.minimum(pre, 0.0) - jnp.log(1.0 + jnp.exp(-jnp.abs(pre)))) / GLA_GATE_NORM
    ch = GLA_CHUNK
    ri = lax.broadcasted_iota(jnp.int32, (ch, ch), 0)
    ci = lax.broadcasted_iota(jnp.int32, (ch, ch), 1)
    tri = jnp.where(ri >= ci, 1.0, 0.0).astype(BF16)
    decs = []
    for c in range(x_ref.shape[0] // ch):
        sl = slice(c * ch, (c + 1) * ch)
        hi, mid, lo = _split3_bf16(g[sl])
        cum = _dot(tri, hi) + _dot(tri, mid) + _dot(tri, lo)
        last = cum[ch - 1:ch]
        qt_ref[sl, :] = (q[sl] * (GLA_DK ** -0.5) * jnp.exp(cum)).astype(BF16)
        kt_ref[sl, :] = (k[sl] * jnp.exp(-cum)).astype(BF16)
        ks_ref[sl, :] = (k[sl] * jnp.exp(last - cum)).astype(BF16)
        decs.append(jnp.exp(last))
    dec_ref[...] = jnp.concatenate(decs, axis=0)


def _gla_gate(x, wq, wk, wa, wa2, ba, tm=512):
    n, d = x.shape
    qk = wq.shape[1]
    row = lambda i: (i, 0)
    whole = lambda i: (0, 0)
    wa_p = jnp.zeros((d, LANES), BF16).at[:, :GLA_RANK].set(wa)
    wa2_p = jnp.zeros((LANES, qk), BF16).at[:GLA_RANK].set(wa2)
    act = jax.ShapeDtypeStruct((n, qk), BF16)
    return pl.pallas_call(
        _gla_gate_kernel,
        grid=(n // tm,),
        in_specs=[pl.BlockSpec((tm, d), row),
                  pl.BlockSpec(wq.shape, whole), pl.BlockSpec(wk.shape, whole),
                  pl.BlockSpec(wa_p.shape, whole), pl.BlockSpec(wa2_p.shape, whole),
                  pl.BlockSpec((1, qk), whole)],
        out_specs=[pl.BlockSpec((tm, qk), row)] * 3 + [pl.BlockSpec((tm // GLA_CHUNK, qk), row)],
        out_shape=[act, act, act, jax.ShapeDtypeStruct((n // GLA_CHUNK, qk), F32)],
        compiler_params=_params("parallel"),
        name="gla_gate",
    )(x, wq, wk, wa_p, wa2_p, ba.reshape(1, qk))


def _gla_core_kernel(qt_ref, kt_ref, ks_ref, v_ref, dec_ref, z_ref, gn_ref, o_ref, s_ref):
    @pl.when(pl.program_id(2) == 0)
    def _():
        s_ref[...] = jnp.zeros_like(s_ref)

    ch = GLA_CHUNK
    dec_t = dec_ref[...].T
    ri = lax.broadcasted_iota(jnp.int32, (ch, ch), 0)
    ci = lax.broadcasted_iota(jnp.int32, (ch, ch), 1)
    tril = ri >= ci
    for c in range(qt_ref.shape[0] // ch):
        sl = slice(c * ch, (c + 1) * ch)
        qt, kt, ks, v = qt_ref[sl, :], kt_ref[sl, :], ks_ref[sl, :], v_ref[sl, :]
        a = jnp.where(tril, lax.dot_general(qt, kt, NT_DIMS, preferred_element_type=F32), 0.0)
        s = s_ref[...]
        o = _dot(a.astype(BF16), v) + _dot(qt, s.astype(BF16))
        s_ref[...] = s * dec_t[:, c:c + 1] + lax.dot_general(ks, v, TN_DIMS, preferred_element_type=F32)
        o = o * lax.rsqrt(jnp.mean(o * o, axis=-1, keepdims=True) + NORM_EPS)
        o = o * gn_ref[...]
        o_ref[sl, :] = (o * _silu(z_ref[sl, :].astype(F32))).astype(BF16)


def _gla_core(qt, kt, ks, vz, dec, gn_g, batch, tc=512):
    n = qt.shape[0]
    seq = n // batch
    tc = min(tc, seq)
    nt = seq // tc
    tok = lambda b, h, i: (b * nt + i, h)
    return pl.pallas_call(
        _gla_core_kernel,
        grid=(batch, GLA_HEADS, nt),
        in_specs=[pl.BlockSpec((tc, GLA_DK), tok)] * 3
                 + [pl.BlockSpec((tc, GLA_DV), tok),
                    pl.BlockSpec((tc // GLA_CHUNK, GLA_DK), tok),
                    pl.BlockSpec((tc, GLA_DV), lambda b, h, i: (b * nt + i, GLA_HEADS + h)),
                    pl.BlockSpec((1, GLA_DV), lambda b, h, i: (0, h))],
        out_specs=pl.BlockSpec((tc, GLA_DV), tok),
        out_shape=jax.ShapeDtypeStruct((n, D_INNER), BF16),
        scratch_shapes=[pltpu.VMEM((GLA_DK, GLA_DV), F32)],
        compiler_params=_params("parallel", "parallel", "arbitrary"),
        name="gla_core",
    )(qt, kt, ks, vz, dec, vz, gn_g.reshape(1, D_INNER))


def _gla_mixer(x, w_in, w_a2, b_a, gn_g, batch):
    qk = GLA_HEADS * GLA_DK
    w = w_in.astype(BF16)
    o_v, o_a = 2 * qk, 2 * qk + 2 * D_INNER
    qt, kt, ks, dec = _gla_gate(x, w[:, :qk], w[:, qk:o_v], w[:, o_a:], w_a2.astype(BF16), b_a)
    vz = _mm(x, w[:, o_v:o_a], BF16)
    return _gla_core(qt, kt, ks, vz, dec, gn_g, batch)


def _rope(x, cos, sin_lo, sin_hi, half):
    return (x * cos + pltpu.roll(x, half, 1) * sin_hi
            + pltpu.roll(x, LANES - half, 1) * sin_lo)


def _rope_tables(pos, invf, half, period):
    ang = pos * invf
    cos, sin = jnp.cos(ang), jnp.sin(ang)
    lane = lax.broadcasted_iota(jnp.int32, ang.shape, 1) % period
    sin_lo = jnp.where(lane < half, -sin, 0.0)
    sin_hi = jnp.where((lane >= half) & (lane < 2 * half), sin, 0.0)
    return cos, sin_lo, sin_hi


def _dsa_prep_kernel(x_ref, pos_ref, invh_ref, invi_ref, wq_ref, wkv_ref, wqi_ref, wki_ref, wwi_ref,
                     q_ref, k_ref, v_ref, qi_ref, ki_ref, wi_ref):
    xb = x_ref[...].astype(BF16)
    pos = pos_ref[...]
    hd = DSA_HEAD_DIM
    cos_h, slo_h, shi_h = _rope_tables(pos, invh_ref[...], DSA_ROT_DIM // 2, hd)
    cos_i, slo_i, shi_i = _rope_tables(pos, invi_ref[...], IDX_ROT_DIM // 2, IDX_DIM)
    q = _dot(xb, wq_ref[...])
    for h in range(DSA_HEADS):
        sl = slice(h * hd, (h + 1) * hd)
        q_ref[:, sl] = (_rope(q[:, sl], cos_h, slo_h, shi_h, DSA_ROT_DIM // 2)
                        * (ATTN_SCALE * LOG2E)).astype(BF16)
    kv = _dot(xb, wkv_ref[...])
    k_ref[...] = _rope(kv[:, :hd], cos_h, slo_h, shi_h, DSA_ROT_DIM // 2).astype(BF16)
    ones_col = jnp.where(lax.broadcasted_iota(jnp.int32, (x_ref.shape[0], LANES), 1) == 0, 1.0, 0.0)
    v_ref[:, :hd] = kv[:, hd:].astype(BF16)
    v_ref[:, hd:] = ones_col.astype(BF16)
    qi = _dot(xb, wqi_ref[...])
    for j in range(qi.shape[1] // LANES):
        sl = slice(j * LANES, (j + 1) * LANES)
        qi_ref[:, sl] = _rope(qi[:, sl], cos_i, slo_i, shi_i, IDX_ROT_DIM // 2).astype(BF16)
    ki = _dot(xb, wki_ref[...])
    ki_ref[...] = _rope(ki, cos_i, slo_i, shi_i, IDX_ROT_DIM // 2).astype(BF16)
    wi_ref[...] = _dot(xb, wwi_ref[...])


def _lane_invfreq(rot_dim, period):
    inv = ROPE_THETA ** (-jnp.arange(0, rot_dim, 2, dtype=F32) / rot_dim)
    head = jnp.concatenate([inv, inv, jnp.zeros((period - rot_dim,), F32)])
    return jnp.tile(head, LANES // period).reshape(1, LANES)


def _pad_cols(w, width):
    return jnp.zeros((w.shape[0], width), w.dtype).at[:, :w.shape[1]].set(w)


def _dsa_prep(x, pos, wq, wkv, wqi, wki, wwi, tm=512):
    n, d = x.shape
    tm = min(tm, n)
    row = lambda i: (i, 0)
    whole = lambda i: (0, 0)
    wki_p, wwi_p = _pad_cols(wki, LANES), _pad_cols(wwi, LANES)
    ws = [wq, wkv, wqi, wki_p, wwi_p]
    return pl.pallas_call(
        _dsa_prep_kernel,
        grid=(n // tm,),
        in_specs=[pl.BlockSpec((tm, d), row), pl.BlockSpec((tm, 1), row),
                  pl.BlockSpec((1, LANES), whole), pl.BlockSpec((1, LANES), whole)]
                 + [pl.BlockSpec(w.shape, whole) for w in ws],
        out_specs=[pl.BlockSpec((tm, D_INNER), row), pl.BlockSpec((tm, LANES), row),
                   pl.BlockSpec((tm, 2 * LANES), row), pl.BlockSpec((tm, IDX_HEADS * IDX_DIM), row),
                   pl.BlockSpec((tm, LANES), row), pl.BlockSpec((tm, LANES), row)],
        out_shape=[jax.ShapeDtypeStruct((n, D_INNER), BF16), jax.ShapeDtypeStruct((n, LANES), BF16),
                   jax.ShapeDtypeStruct((n, 2 * LANES), BF16),
                   jax.ShapeDtypeStruct((n, IDX_HEADS * IDX_DIM), BF16),
                   jax.ShapeDtypeStruct((n, LANES), BF16), jax.ShapeDtypeStruct((n, LANES), F32)],
        compiler_params=_params("parallel"),
        name="dsa_prep",
    )(x, pos, _lane_invfreq(DSA_ROT_DIM, DSA_HEAD_DIM), _lane_invfreq(IDX_ROT_DIM, IDX_DIM), *ws)


COUNT_ROWS = 64
MASKED = -1e30
QSUB = 128


def _sortable_key(score):
    b = pltpu.bitcast(score, jnp.int32)
    key = b ^ ((b >> 31) & 0x7FFFFFFF)
    return jnp.where(score == 0.0, 0, key)


def _dsa_core_kernel(topk, tq, sc, hg,
                     q_ref, qi_ref, wi_ref, z_ref, k_ref, va_ref, ki_ref, o_ref,
                     key_ref, hi_ref, lo_ref, kx_ref, qx_ref, os_ref, s_ref, mx_ref, m_ref, acc_ref):
    qb = pl.program_id(1)
    t0 = qb * tq
    nkc = (t0 + tq + sc - 1) // sc
    hd = DSA_HEAD_DIM
    nsub = tq // QSUB
    groups = DSA_HEADS // hg
    rows = hg * QSUB

    @pl.when(qb == 0)
    def _():
        for a in range(nsub):
            kx_ref[a, :, :hd] = k_ref[...]
        ri = lax.broadcasted_iota(jnp.int32, (QSUB, LANES), 0)
        ci = lax.broadcasted_iota(jnp.int32, (QSUB, LANES), 1)
        eye = jnp.where(ri == ci, 1.0, 0.0).astype(BF16)
        for j in range(nsub * DSA_HEADS):
            qx_ref[j * QSUB:(j + 1) * QSUB, hd:] = eye

    qi_s = jnp.concatenate([qi_ref[:, h * IDX_DIM:(h + 1) * IDX_DIM] for h in range(IDX_HEADS)], axis=0)
    wi_t = wi_ref[...].T
    wi_row = jnp.concatenate([wi_t[h:h + 1, :] for h in range(IDX_HEADS)], axis=1)
    k_pos = lax.broadcasted_iota(jnp.int32, (sc, tq), 0)
    t_pos = t0 + lax.broadcasted_iota(jnp.int32, (sc, tq), 1)

    def score_body(c, carry):
        ki_c = ki_ref[pl.ds(pl.multiple_of(c * sc, sc), sc), :][:, :IDX_DIM]
        rel = jnp.maximum(lax.dot_general(ki_c, qi_s, NT_DIMS, preferred_element_type=F32), 0.0) * wi_row
        score = rel[:, :tq]
        for h in range(1, IDX_HEADS):
            score = score + rel[:, h * tq:(h + 1) * tq]
        score = score * INDEX_SCALE
        key = jnp.where(c * sc + k_pos <= t_pos, _sortable_key(score), INT_MIN)
        key_ref[c] = key
        hi_ref[c] = (key >> 16).astype(jnp.int16)
        return carry

    lax.fori_loop(0, nkc, score_body, 0)

    def count16(ref, pred):
        def body(c, acc):
            m = jnp.where(pred(ref[c]), jnp.bfloat16(1), jnp.bfloat16(0))
            for j in range(sc // COUNT_ROWS):
                acc = acc + m[j * COUNT_ROWS:(j + 1) * COUNT_ROWS]
            return acc
        acc = lax.fori_loop(0, nkc, body, jnp.zeros((COUNT_ROWS, tq), BF16))
        return jnp.sum(acc.astype(F32), axis=0, keepdims=True)

    def search16(ref, need):
        def body(i, r):
            cand = r | (jnp.int32(1) << (15 - i))
            cand16 = (cand - 32768).astype(jnp.int16)
            return jnp.where(count16(ref, lambda v: v >= cand16) >= need, cand, r)
        return lax.fori_loop(0, 16, body, jnp.zeros((1, tq), jnp.int32)) - 32768

    t_hi = search16(hi_ref, topk)
    t_hi16 = t_hi.astype(jnp.int16)
    n_hi_gt = count16(hi_ref, lambda v: v > t_hi16)

    def lo_body(c, carry):
        key = key_ref[c]
        lo_ref[c] = jnp.where((key >> 16) == t_hi, (key & 0xFFFF) - 32768, -32768).astype(jnp.int16)
        return carry

    lax.fori_loop(0, nkc, lo_body, 0)
    t_lo = search16(lo_ref, topk - n_hi_gt)
    thr = (t_hi << 16) | (t_lo + 32768)

    def count(pred_fn):
        def body(c, acc):
            m = jnp.where(pred_fn(key_ref[c], c), 1.0, 0.0)
            return acc + jnp.sum(m.reshape(sc // COUNT_ROWS, COUNT_ROWS, tq), axis=0)
        acc = lax.fori_loop(0, nkc, body, jnp.zeros((COUNT_ROWS, tq), F32))
        return jnp.sum(acc, axis=0, keepdims=True)

    n_gt = count(lambda key, c: key > thr)
    n_ge = count(lambda key, c: key >= thr)
    need = topk - n_gt
    has_thr = thr != INT_MIN
    excess = jnp.max(jnp.where(has_thr, n_ge - n_gt - need, 0.0))

    def tie_limit():
        def body(i, r):
            cand = r | (jnp.int32(1) << (15 - i))
            n = count(lambda key, c: (key == thr) & (c * sc + k_pos < cand))
            return jnp.where(n < need, cand, r)
        return lax.fori_loop(0, 16, body, jnp.zeros((1, tq), jnp.int32))

    limit = lax.cond(excess > 0.0, tie_limit, lambda: jnp.full((1, tq), 2 ** 30, jnp.int32))
    limit = jnp.where(has_thr, limit, -1)

    def mask_body(c, carry):
        key = key_ref[c]
        sel = (key > thr) | ((key == thr) & (c * sc + k_pos <= limit))
        mask = jnp.where(sel, 0.0, MASKED).astype(BF16)
        k0 = pl.multiple_of(c * sc, sc)
        for a in range(nsub):
            kx_ref[a, pl.ds(k0, sc), hd:] = mask[:, a * QSUB:(a + 1) * QSUB]
        return carry

    lax.fori_loop(0, nkc, mask_body, 0)

    for a in range(nsub):
        for h in range(DSA_HEADS):
            j = a * DSA_HEADS + h
            qx_ref[j * QSUB:(j + 1) * QSUB, :hd] = q_ref[a * QSUB:(a + 1) * QSUB, h * hd:(h + 1) * hd]

    def logits(u, c):
        k0 = pl.multiple_of(c * sc, sc)
        s = lax.dot_general(qx_ref[u * rows:(u + 1) * rows, :], kx_ref[u // groups, pl.ds(k0, sc), :],
                            NT_DIMS, preferred_element_type=F32)
        s_ref[c] = s
        mx = mx_ref[...]
        for j in range(sc // LANES):
            mx = jnp.maximum(mx, s[:, j * LANES:(j + 1) * LANES])
        mx_ref[...] = mx

    def weighted(c):
        k0 = pl.multiple_of(c * sc, sc)
        m = m_ref[...]
        p = jnp.concatenate([jnp.exp2(s_ref[c, :, j * LANES:(j + 1) * LANES] - m)
                             for j in range(sc // LANES)], axis=1)
        acc_ref[...] += _dot(p.astype(BF16), va_ref[pl.ds(k0, sc), :])

    def start_unit():
        m_ref[...] = jnp.broadcast_to(jnp.max(mx_ref[...], axis=-1, keepdims=True), m_ref.shape)
        mx_ref[...] = jnp.full(mx_ref.shape, -jnp.inf, F32)
        acc_ref[...] = jnp.zeros(acc_ref.shape, F32)

    def finish(u):
        acc = acc_ref[...]
        os_ref[u * rows:(u + 1) * rows, :] = acc[:, :hd] / acc[:, hd:hd + 1]

    def run(body):
        lax.fori_loop(0, nkc, lambda c, carry: (body(c), carry)[1], 0)

    units = nsub * groups
    mx_ref[...] = jnp.full(mx_ref.shape, -jnp.inf, F32)
    run(lambda c: logits(0, c))
    for u in range(1, units):
        start_unit()
        run(lambda c, u=u: (weighted(c), logits(u, c)))
        finish(u - 1)
    start_unit()
    run(weighted)
    finish(units - 1)

    for a in range(nsub):
        for h in range(DSA_HEADS):
            j = a * DSA_HEADS + h
            qr, sl = slice(a * QSUB, (a + 1) * QSUB), slice(h * hd, (h + 1) * hd)
            o_ref[qr, sl] = (os_ref[j * QSUB:(j + 1) * QSUB, :] * _silu(z_ref[qr, sl].astype(F32))).astype(BF16)


def _dsa_core(q, qi, wi, z, k, va, ki, batch, tq=256, sc=512, hg=8):
    n = q.shape[0]
    seq = n // batch
    tq, sc = min(tq, seq), min(sc, seq)
    nq = seq // tq
    topk = min(TOPK_MAX, seq // 4)
    assert seq // COUNT_ROWS <= 256 and seq <= 2 ** 16 and tq % QSUB == 0
    stacked = (tq // QSUB) * DSA_HEADS * QSUB
    blk = lambda b, i: (b * nq + i, 0)
    full = lambda b, i: (b, 0)
    return pl.pallas_call(
        functools.partial(_dsa_core_kernel, topk, tq, sc, hg),
        grid=(batch, nq),
        in_specs=[pl.BlockSpec((tq, D_INNER), blk), pl.BlockSpec((tq, IDX_HEADS * IDX_DIM), blk),
                  pl.BlockSpec((tq, LANES), blk), pl.BlockSpec((tq, D_INNER), blk),
                  pl.BlockSpec((seq, LANES), full), pl.BlockSpec((seq, 2 * LANES), full),
                  pl.BlockSpec((seq, LANES), full)],
        out_specs=pl.BlockSpec((tq, D_INNER), blk),
        out_shape=jax.ShapeDtypeStruct((n, D_INNER), BF16),
        scratch_shapes=[pltpu.VMEM((seq // sc, sc, tq), jnp.int32),
                        pltpu.VMEM((seq // sc, sc, tq), jnp.int16),
                        pltpu.VMEM((seq // sc, sc, tq), jnp.int16),
                        pltpu.VMEM((tq // QSUB, seq, 2 * LANES), BF16),
                        pltpu.VMEM((stacked, 2 * LANES), BF16),
                        pltpu.VMEM((stacked, DSA_HEAD_DIM), F32),
                        pltpu.VMEM((seq // sc, hg * QSUB, sc), F32),
                        pltpu.VMEM((hg * QSUB, LANES), F32),
                        pltpu.VMEM((hg * QSUB, LANES), F32),
                        pltpu.VMEM((hg * QSUB, 2 * LANES), F32)],
        compiler_params=_params("parallel", "arbitrary"),
        name="dsa_core",
    )(q, qi, wi, z, k, va, ki)


def _dsa_mixer(x, pos, w_in, batch):
    w = w_in.astype(BF16)
    hd = DSA_HEAD_DIM
    o1 = D_INNER
    o3 = o1 + 2 * hd
    o4 = o3 + D_INNER
    o5 = o4 + IDX_HEADS * IDX_DIM
    o6 = o5 + IDX_DIM
    q, k, va, qi, ki, wi = _dsa_prep(x, pos, w[:, :o1], w[:, o1:o3], w[:, o4:o5], w[:, o5:o6], w[:, o6:])
    z = _mm(x, w[:, o3:o4], BF16)
    return _dsa_core(q, qi, wi, z, k, va, ki, batch)


CONV_HALO = 32
CONV_ROWS = 64
CONV_COLS = 256
SUBLANES = 8
NORM_ROWS = 128


def _conv_core_kernel(u_ref, z_ref, w_ref, b_ref, lng_ref, lnb_ref, o_ref,
                      buf_ref, sh_ref, conv_ref):
    t, ch = u_ref.shape

    @pl.when(pl.program_id(1) == 0)
    def _():
        buf_ref[0:CONV_HALO, :] = jnp.zeros((CONV_HALO, ch), F32)

    buf_ref[CONV_HALO:CONV_HALO + t, :] = u_ref[...]
    first = CONV_HALO - (CONV_WIDTH - 1)
    span = t + CONV_HALO - SUBLANES
    for cb in range(ch // CONV_COLS):
        cols = slice(cb * CONV_COLS, (cb + 1) * CONV_COLS)
        for s in range(1, SUBLANES):
            sh_ref[s - 1] = buf_ref[s:s + span, cols]

        def tile(i, carry, cols=cols):
            r0 = pl.multiple_of(i * CONV_ROWS, CONV_ROWS)
            acc = jnp.zeros((CONV_ROWS, CONV_COLS), F32)
            for j in range(CONV_WIDTH):
                s = (first + j) % SUBLANES
                base = r0 + (first + j - s)
                if s == 0:
                    src = buf_ref[pl.ds(base, CONV_ROWS), cols]
                else:
                    src = sh_ref[s - 1, pl.ds(base, CONV_ROWS), :]
                acc = acc + src * w_ref[j:j + 1, cols]
            conv_ref[pl.ds(r0, CONV_ROWS), cols] = acc + b_ref[:, cols]
            return carry

        lax.fori_loop(0, t // CONV_ROWS, tile, 0)

    def norm_tile(i, carry):
        rows = pl.ds(pl.multiple_of(i * NORM_ROWS, NORM_ROWS), NORM_ROWS)
        u = conv_ref[rows, :]
        mu = jnp.mean(u, axis=-1, keepdims=True)
        d = u - mu
        var = jnp.mean(d * d, axis=-1, keepdims=True)
        u = _silu(d * lax.rsqrt(var + NORM_EPS) * lng_ref[...] + lnb_ref[...])
        o_ref[rows, :] = (u * _silu(z_ref[rows, :].astype(F32))).astype(BF16)
        return carry

    lax.fori_loop(0, t // NORM_ROWS, norm_tile, 0)
    buf_ref[0:CONV_HALO, :] = buf_ref[t:t + CONV_HALO, :]


def _conv_core(u, z, dw_w, dw_b, ln_g, ln_b, batch, t=256):
    n, c = u.shape
    seq = n // batch
    t = min(t, seq)
    nt = seq // t
    blk = lambda b, i: (b * nt + i, 0)
    whole = lambda b, i: (0, 0)
    w_p = jnp.zeros((CONV_HALO, c), F32).at[:CONV_WIDTH].set(dw_w)
    vec = pl.BlockSpec((1, c), whole)
    return pl.pallas_call(
        _conv_core_kernel,
        grid=(batch, nt),
        in_specs=[pl.BlockSpec((t, c), blk)] * 2 + [pl.BlockSpec(w_p.shape, whole), vec, vec, vec],
        out_specs=pl.BlockSpec((t, c), blk),
        out_shape=jax.ShapeDtypeStruct((n, c), BF16),
        scratch_shapes=[pltpu.VMEM((t + CONV_HALO, c), F32),
                        pltpu.VMEM((SUBLANES - 1, t + CONV_HALO - SUBLANES, CONV_COLS), F32),
                        pltpu.VMEM((t, c), F32)],
        compiler_params=_params("parallel", "arbitrary"),
        name="conv_core",
    )(u, z, w_p, dw_b.reshape(1, c), ln_g.reshape(1, c), ln_b.reshape(1, c))


def _conv_mixer(x, w_in, dw_w, dw_b, ln_g, ln_b, batch):
    w = w_in.astype(BF16)
    u = _mm_glu(x, w[:, :D_INNER], w[:, D_INNER:2 * D_INNER])
    z = _mm(x, w[:, 2 * D_INNER:], BF16)
    return _conv_core(u, z, dw_w, dw_b, ln_g, ln_b, batch)


def kernel(x, p, positions, gla_w_in, gla_w_a2, gla_b_a, gla_gn_g, gla_w_out, dsa_w_in, dsa_w_out,
           conv_w_in, conv_dw_w, conv_dw_b, conv_ln_g, conv_ln_b, conv_w_out, ln_g, ln_b, ple_w,
           ple_gate_w):
    batch, seq, d = x.shape
    depth = p.shape[0]
    n = batch * seq
    alpha = (2 * depth) ** 0.25
    xf = x.reshape(n, d)
    xb = xf
    pos = positions.astype(F32).reshape(n, 1)
    for i in range(depth):
        kind, j = i % N_MIXERS, i // N_MIXERS
        if kind == 0:
            u = _gla_mixer(xb, gla_w_in[j], gla_w_a2[j], gla_b_a[j], gla_gn_g[j], batch)
            w_out = gla_w_out[j]
        elif kind == 1:
            u = _dsa_mixer(xb, pos, dsa_w_in[j], batch)
            w_out = dsa_w_out[j]
        else:
            u = _conv_mixer(xb, conv_w_in[j], conv_dw_w[j], conv_dw_b[j], conv_ln_g[j], conv_ln_b[j], batch)
            w_out = conv_w_out[j]
        xf, xb = _post(u, xf, p[i].reshape(n, -1), w_out, ln_g[i], ln_b[i], ple_w[i], ple_gate_w[i], alpha)
    return xf.reshape(batch, seq, d)
```

```python
import functools

import jax
import jax.numpy as jnp
from jax import lax
from jax.experimental import pallas as pl
from jax.experimental.pallas import tpu as pltpu

F32 = jnp.float32
BF16 = jnp.bfloat16

D_MODEL = 1024
N_MIXERS = 3
D_INNER = 2 * D_MODEL
GLA_HEADS = 4
GLA_DK = D_MODEL // GLA_HEADS
GLA_DV = D_INNER // GLA_HEADS
GLA_RANK = 16
GLA_GATE_NORM = 16.0
GLA_CHUNK = 64
DSA_HEADS = 16
DSA_HEAD_DIM = D_INNER // DSA_HEADS
IDX_HEADS = 8
IDX_DIM = 64
TOPK_MAX = 256
ATTN_SCALE = DSA_HEAD_DIM ** -0.5
INDEX_SCALE = (IDX_HEADS ** -0.5) * (IDX_DIM ** -0.5)
CONV_WIDTH = 31
ROPE_THETA = 500000.0
DSA_ROT_DIM = DSA_HEAD_DIM // 4
IDX_ROT_DIM = IDX_DIM // 4
PLE_DIM = 256
NORM_EPS = 1e-5

LANES = 128
LOG2E = 1.4426950408889634
INT_MIN = -(2 ** 31)
VMEM_LIMIT = 56 * 1024 * 1024

NT_DIMS = (((1,), (1,)), ((), ()))
TN_DIMS = (((0,), (0,)), ((), ()))


def _params(*sem):
    return pltpu.CompilerParams(dimension_semantics=sem, vmem_limit_bytes=VMEM_LIMIT)


def _dot(a, b):
    return jnp.dot(a, b, preferred_element_type=F32)


def _silu(x):
    return x * jax.nn.sigmoid(x)


def _mm_kernel(x_ref, w_ref, o_ref):
    o_ref[...] = _dot(x_ref[...].astype(BF16), w_ref[...]).astype(o_ref.dtype)


def _mm(x, w, out_dtype, col0=0, cols=None, tm=1024, tn=1024):
    n, k = x.shape
    c = w.shape[1] - col0 if cols is None else cols
    tm, tn = min(tm, n), min(tn, c)
    assert col0 % tn == 0 and c % tn == 0
    j0 = col0 // tn
    return pl.pallas_call(
        _mm_kernel,
        grid=(n // tm, c // tn),
        in_specs=[pl.BlockSpec((tm, k), lambda i, j: (i, 0)),
                  pl.BlockSpec((k, tn), lambda i, j: (0, j0 + j))],
        out_specs=pl.BlockSpec((tm, tn), lambda i, j: (i, j)),
        out_shape=jax.ShapeDtypeStruct((n, c), out_dtype),
        compiler_params=_params("parallel", "parallel"),
        name="proj",
    )(x, w)


def _mm_glu_kernel(x_ref, wa_ref, wg_ref, o_ref):
    xb = x_ref[...].astype(BF16)
    o_ref[...] = _dot(xb, wa_ref[...]) * jax.nn.sigmoid(_dot(xb, wg_ref[...]))


def _mm_glu(x, w, c, tm=1024, tn=512):
    n, k = x.shape
    tm, tn = min(tm, n), min(tn, c)
    assert c % tn == 0
    nj = c // tn
    return pl.pallas_call(
        _mm_glu_kernel,
        grid=(n // tm, nj),
        in_specs=[pl.BlockSpec((tm, k), lambda i, j: (i, 0)),
                  pl.BlockSpec((k, tn), lambda i, j: (0, j)),
                  pl.BlockSpec((k, tn), lambda i, j: (0, nj + j))],
        out_specs=pl.BlockSpec((tm, tn), lambda i, j: (i, j)),
        out_shape=jax.ShapeDtypeStruct((n, c), F32),
        compiler_params=_params("parallel", "parallel"),
        name="proj_glu",
    )(x, w, w)


def _post_kernel(alpha, u_ref, x_ref, p_ref, wout_ref, lng_ref, lnb_ref, plew_ref, gatew_ref, o_ref, ob_ref):
    y = _dot(u_ref[...], wout_ref[...])
    ple = _dot(p_ref[...].astype(BF16), plew_ref[...])
    r = alpha * x_ref[...] + y
    mu = jnp.mean(r, axis=-1, keepdims=True)
    d = r - mu
    var = jnp.mean(d * d, axis=-1, keepdims=True)
    x1 = d * lax.rsqrt(var + NORM_EPS) * lng_ref[...] + lnb_ref[...]
    gate = _dot(x1.astype(BF16), gatew_ref[...])
    out = x1 + ple * jax.nn.sigmoid(gate)
    o_ref[...] = out
    ob_ref[...] = out.astype(BF16)


def _post(u, x, p_all, layer, w_out, ln_g, ln_b, ple_w, gate_w, alpha, tm=512):
    n, d = x.shape
    tm = min(tm, n)
    row = lambda i: (i, 0)
    whole = lambda i: (0, 0)
    p0 = layer * (n // tm)
    return pl.pallas_call(
        functools.partial(_post_kernel, alpha),
        grid=(n // tm,),
        in_specs=[pl.BlockSpec((tm, u.shape[1]), row),
                  pl.BlockSpec((tm, d), row),
                  pl.BlockSpec((tm, p_all.shape[1]), lambda i: (p0 + i, 0)),
                  pl.BlockSpec(w_out.shape, whole),
                  pl.BlockSpec((1, d), whole),
                  pl.BlockSpec((1, d), whole),
                  pl.BlockSpec(ple_w.shape, whole),
                  pl.BlockSpec(gate_w.shape, whole)],
        out_specs=[pl.BlockSpec((tm, d), row)] * 2,
        out_shape=[jax.ShapeDtypeStruct((n, d), F32), jax.ShapeDtypeStruct((n, d), BF16)],
        compiler_params=_params("parallel"),
        name="post",
    )(u, x, p_all, w_out.astype(BF16), ln_g.reshape(1, d), ln_b.reshape(1, d),
      ple_w.astype(BF16), gate_w.astype(BF16))


def _split3_bf16(x):
    hi = x.astype(BF16)
    r = x - hi.astype(F32)
    mid = r.astype(BF16)
    lo = (r - mid.astype(F32)).astype(BF16)
    return hi, mid, lo


def _gla_gate_kernel(x_ref, wq_ref, wk_ref, wa_ref, wa2_ref, ba_ref,
                     qt_ref, kt_ref, ks_ref, dec_ref):
    xb = x_ref[...].astype(BF16)
    a = _dot(xb, wa_ref[...])
    pre = _dot(a.astype(BF16), wa2_ref[...]) + ba_ref[...]
    g = (jnp.minimum(pre, 0.0) - jnp.log(1.0 + jnp.exp(-jnp.abs(pre)))) / GLA_GATE_NORM
    ch = GLA_CHUNK
    ri = lax.broadcasted_iota(jnp.int32, (ch, ch), 0)
    ci = lax.broadcasted_iota(jnp.int32, (ch, ch), 1)
    tri = jnp.where(ri >= ci, 1.0, 0.0).astype(BF16)
    chunks = [slice(c * ch, (c + 1) * ch) for c in range(x_ref.shape[0] // ch)]
    cums = []
    for sl in chunks:
        hi, mid, lo = _split3_bf16(g[sl])
        cums.append(_dot(tri, hi) + _dot(tri, mid) + _dot(tri, lo))
    q = _dot(xb, wq_ref[...])
    for sl, cum in zip(chunks, cums):
        qt_ref[sl, :] = (q[sl] * (GLA_DK ** -0.5) * jnp.exp(cum)).astype(BF16)
    k = _dot(xb, wk_ref[...])
    for sl, cum in zip(chunks, cums):
        last = cum[ch - 1:ch]
        kt_ref[sl, :] = (k[sl] * jnp.exp(-cum)).astype(BF16)
        ks_ref[sl, :] = (k[sl] * jnp.exp(last - cum)).astype(BF16)
    dec_ref[...] = jnp.concatenate([jnp.exp(cum[ch - 1:ch]) for cum in cums], axis=0)


def _gla_gate(x, w, wa, wa2, ba, tm=512):
    n, d = x.shape
    qk = GLA_HEADS * GLA_DK
    row = lambda i: (i, 0)
    whole = lambda i: (0, 0)
    wa_p = jnp.zeros((d, LANES), BF16).at[:, :GLA_RANK].set(wa)
    wa2_p = jnp.zeros((LANES, qk), BF16).at[:GLA_RANK].set(wa2)
    act = jax.ShapeDtypeStruct((n, qk), BF16)
    return pl.pallas_call(
        _gla_gate_kernel,
        grid=(n // tm,),
        in_specs=[pl.BlockSpec((tm, d), row),
                  pl.BlockSpec((d, qk), lambda i: (0, 0)), pl.BlockSpec((d, qk), lambda i: (0, 1)),
                  pl.BlockSpec(wa_p.shape, whole), pl.BlockSpec(wa2_p.shape, whole),
                  pl.BlockSpec((1, qk), whole)],
        out_specs=[pl.BlockSpec((tm, qk), row)] * 3 + [pl.BlockSpec((tm // GLA_CHUNK, qk), row)],
        out_shape=[act, act, act, jax.ShapeDtypeStruct((n // GLA_CHUNK, qk), F32)],
        compiler_params=_params("parallel"),
        name="gla_gate",
    )(x, w, w, wa_p, wa2_p, ba.reshape(1, qk))


def _gla_core_kernel(qt_ref, kt_ref, ks_ref, v_ref, dec_ref, z_ref, gn_ref, o_ref, s_ref, sb_ref, oi_ref):
    @pl.when(pl.program_id(2) == 0)
    def _():
        s_ref[...] = jnp.zeros_like(s_ref)
        sb_ref[...] = jnp.zeros_like(sb_ref)

    ch, dk, dv = GLA_CHUNK, GLA_DK, GLA_DV
    ri = lax.broadcasted_iota(jnp.int32, (ch, ch), 0)
    ci = lax.broadcasted_iota(jnp.int32, (ch, ch), 1)
    tril = ri >= ci
    dec_t = [dec_ref[:, j * dk:(j + 1) * dk].T for j in range(GLA_HEADS_PER_STEP)]
    chunks = [(c, j) for c in range(qt_ref.shape[0] // ch) for j in range(GLA_HEADS_PER_STEP)]

    def tiles(c, j):
        return slice(c * ch, (c + 1) * ch), slice(j * dk, (j + 1) * dk), slice(j * dv, (j + 1) * dv)

    a = {}
    for c, j in chunks:
        sl, kc, _ = tiles(c, j)
        a[c, j] = jnp.where(tril, lax.dot_general(qt_ref[sl, kc], kt_ref[sl, kc], NT_DIMS,
                                                  preferred_element_type=F32), 0.0).astype(BF16)
    for c, j in chunks:
        sl, _, vc = tiles(c, j)
        oi_ref[sl, vc] = _dot(a[c, j], v_ref[sl, vc])
    for c, j in chunks:
        sl, kc, vc = tiles(c, j)
        v = v_ref[sl, vc]
        o = oi_ref[sl, vc] + _dot(qt_ref[sl, kc], sb_ref[j])
        s = s_ref[j] * dec_t[j][:, c:c + 1] + lax.dot_general(ks_ref[sl, kc], v, TN_DIMS, preferred_element_type=F32)
        s_ref[j] = s
        sb_ref[j] = s.astype(BF16)
        o = o * lax.rsqrt(jnp.mean(o * o, axis=-1, keepdims=True) + NORM_EPS)
        o = o * gn_ref[:, vc]
        o_ref[sl, vc] = (o * _silu(z_ref[sl, vc].astype(F32))).astype(BF16)


GLA_HEADS_PER_STEP = 2


def _gla_core(qt, kt, ks, vz, dec, gn_g, batch, tc=512):
    n = qt.shape[0]
    seq = n // batch
    tc = min(tc, seq)
    nt = seq // tc
    hb = GLA_HEADS_PER_STEP
    groups = GLA_HEADS // hb
    tok = lambda b, g, i: (b * nt + i, g)
    return pl.pallas_call(
        _gla_core_kernel,
        grid=(batch, groups, nt),
        in_specs=[pl.BlockSpec((tc, hb * GLA_DK), tok)] * 3
                 + [pl.BlockSpec((tc, hb * GLA_DV), tok),
                    pl.BlockSpec((tc // GLA_CHUNK, hb * GLA_DK), tok),
                    pl.BlockSpec((tc, hb * GLA_DV), lambda b, g, i: (b * nt + i, groups + g)),
                    pl.BlockSpec((1, hb * GLA_DV), lambda b, g, i: (0, g))],
        out_specs=pl.BlockSpec((tc, hb * GLA_DV), tok),
        out_shape=jax.ShapeDtypeStruct((n, D_INNER), BF16),
        scratch_shapes=[pltpu.VMEM((hb, GLA_DK, GLA_DV), F32), pltpu.VMEM((hb, GLA_DK, GLA_DV), BF16),
                        pltpu.VMEM((tc, hb * GLA_DV), F32)],
        compiler_params=_params("parallel", "parallel", "arbitrary"),
        name="gla_core",
    )(qt, kt, ks, vz, dec, vz, gn_g.reshape(1, D_INNER))


def _gla_mixer(x, w_in, w_a2, b_a, gn_g, batch):
    qk = GLA_HEADS * GLA_DK
    w = w_in.astype(BF16)
    o_v, o_a = 2 * qk, 2 * qk + 2 * D_INNER
    qt, kt, ks, dec = _gla_gate(x, w, w[:, o_a:], w_a2.astype(BF16), b_a)
    vz = _mm(x, w, BF16, col0=o_v, cols=o_a - o_v)
    return _gla_core(qt, kt, ks, vz, dec, gn_g, batch)


def _rope(x, cos, sin_lo, sin_hi, half):
    return (x * cos + pltpu.roll(x, half, 1) * sin_hi
            + pltpu.roll(x, LANES - half, 1) * sin_lo)


def _rope_tables(pos, invf, half, period):
    ang = pos * invf
    cos, sin = jnp.cos(ang), jnp.sin(ang)
    lane = lax.broadcasted_iota(jnp.int32, ang.shape, 1) % period
    sin_lo = jnp.where(lane < half, -sin, 0.0)
    sin_hi = jnp.where((lane >= half) & (lane < 2 * half), sin, 0.0)
    return cos, sin_lo, sin_hi


def _dsa_prep_kernel(x_ref, pos_ref, invh_ref, invi_ref, wq_ref, wkv_ref, wqi_ref, wki_ref, wwi_ref,
                     q_ref, k_ref, v_ref, qi_ref, ki_ref, wi_ref):
    xb = x_ref[...].astype(BF16)
    pos = pos_ref[...]
    hd = DSA_HEAD_DIM
    cos_h, slo_h, shi_h = _rope_tables(pos, invh_ref[...], DSA_ROT_DIM // 2, hd)
    cos_i, slo_i, shi_i = _rope_tables(pos, invi_ref[...], IDX_ROT_DIM // 2, IDX_DIM)
    q = _dot(xb, wq_ref[...])
    for h in range(DSA_HEADS):
        sl = slice(h * hd, (h + 1) * hd)
        q_ref[:, sl] = (_rope(q[:, sl], cos_h, slo_h, shi_h, DSA_ROT_DIM // 2)
                        * (ATTN_SCALE * LOG2E)).astype(BF16)
    kv = _dot(xb, wkv_ref[...])
    k_ref[...] = _rope(kv[:, :hd], cos_h, slo_h, shi_h, DSA_ROT_DIM // 2).astype(BF16)
    ones_col = jnp.where(lax.broadcasted_iota(jnp.int32, (x_ref.shape[0], LANES), 1) == 0, 1.0, 0.0)
    v_ref[:, :hd] = kv[:, hd:].astype(BF16)
    v_ref[:, hd:] = ones_col.astype(BF16)
    qi = _dot(xb, wqi_ref[...])
    for j in range(qi.shape[1] // LANES):
        sl = slice(j * LANES, (j + 1) * LANES)
        qi_ref[:, sl] = _rope(qi[:, sl], cos_i, slo_i, shi_i, IDX_ROT_DIM // 2).astype(BF16)
    ki = _dot(xb, wki_ref[...])
    ki_ref[...] = _rope(ki, cos_i, slo_i, shi_i, IDX_ROT_DIM // 2).astype(BF16)
    wi_ref[...] = _dot(xb, wwi_ref[...])


def _lane_invfreq(rot_dim, period):
    inv = ROPE_THETA ** (-jnp.arange(0, rot_dim, 2, dtype=F32) / rot_dim)
    head = jnp.concatenate([inv, inv, jnp.zeros((period - rot_dim,), F32)])
    return jnp.tile(head, LANES // period).reshape(1, LANES)


def _pad_cols(w, width):
    return jnp.zeros((w.shape[0], width), w.dtype).at[:, :w.shape[1]].set(w)


def _dsa_prep(x, pos, wq, wkv, wqi, wki, wwi, tm=512):
    n, d = x.shape
    tm = min(tm, n)
    row = lambda i: (i, 0)
    whole = lambda i: (0, 0)
    wki_p, wwi_p = _pad_cols(wki, LANES), _pad_cols(wwi, LANES)
    ws = [wq, wkv, wqi, wki_p, wwi_p]
    return pl.pallas_call(
        _dsa_prep_kernel,
        grid=(n // tm,),
        in_specs=[pl.BlockSpec((tm, d), row), pl.BlockSpec((tm, 1), row),
                  pl.BlockSpec((1, LANES), whole), pl.BlockSpec((1, LANES), whole)]
                 + [pl.BlockSpec(w.shape, whole) for w in ws],
        out_specs=[pl.BlockSpec((tm, D_INNER), row), pl.BlockSpec((tm, LANES), row),
                   pl.BlockSpec((tm, 2 * LANES), row), pl.BlockSpec((tm, IDX_HEADS * IDX_DIM), row),
                   pl.BlockSpec((tm, LANES), row), pl.BlockSpec((tm, LANES), row)],
        out_shape=[jax.ShapeDtypeStruct((n, D_INNER), BF16), jax.ShapeDtypeStruct((n, LANES), BF16),
                   jax.ShapeDtypeStruct((n, 2 * LANES), BF16),
                   jax.ShapeDtypeStruct((n, IDX_HEADS * IDX_DIM), BF16),
                   jax.ShapeDtypeStruct((n, LANES), BF16), jax.ShapeDtypeStruct((n, LANES), F32)],
        compiler_params=_params("parallel"),
        name="dsa_prep",
    )(x, pos, _lane_invfreq(DSA_ROT_DIM, DSA_HEAD_DIM), _lane_invfreq(IDX_ROT_DIM, IDX_DIM), *ws)


COUNT_ROWS = 64
MASKED = -1e30
QSUB = 128


def _sortable_key(score):
    b = pltpu.bitcast(score, jnp.int32)
    key = b ^ ((b >> 31) & 0x7FFFFFFF)
    return jnp.where(score == 0.0, 0, key)


def _dsa_core_kernel(topk, tq, sc, hg,
                     q_ref, qi_ref, wi_ref, z_ref, k_ref, va_ref, ki_ref, o_ref,
                     key_ref, hi_ref, lo_ref, kx_ref, qx_ref, os_ref, s_ref, mx_ref, m_ref, acc_ref):
    qb = pl.program_id(1)
    t0 = qb * tq
    nkc = (t0 + tq + sc - 1) // sc
    hd = DSA_HEAD_DIM
    nsub = tq // QSUB
    groups = DSA_HEADS // hg
    rows = hg * QSUB

    @pl.when(qb == 0)
    def _():
        for a in range(nsub):
            kx_ref[a, :, :hd] = k_ref[...]
        ri = lax.broadcasted_iota(jnp.int32, (QSUB, LANES), 0)
        ci = lax.broadcasted_iota(jnp.int32, (QSUB, LANES), 1)
        eye = jnp.where(ri == ci, 1.0, 0.0).astype(BF16)
        for j in range(nsub * DSA_HEADS):
            qx_ref[j * QSUB:(j + 1) * QSUB, hd:] = eye

    qi_s = jnp.concatenate([qi_ref[:, h * IDX_DIM:(h + 1) * IDX_DIM] for h in range(IDX_HEADS)], axis=0)
    wi_t = wi_ref[...].T
    wi_row = jnp.concatenate([wi_t[h:h + 1, :] for h in range(IDX_HEADS)], axis=1)
    k_pos = lax.broadcasted_iota(jnp.int32, (sc, tq), 0)
    t_pos = t0 + lax.broadcasted_iota(jnp.int32, (sc, tq), 1)

    def score_body(c, carry):
        ki_c = ki_ref[pl.ds(pl.multiple_of(c * sc, sc), sc), :][:, :IDX_DIM]
        rel = jnp.maximum(lax.dot_general(ki_c, qi_s, NT_DIMS, preferred_element_type=F32), 0.0) * wi_row
        score = rel[:, :tq]
        for h in range(1, IDX_HEADS):
            score = score + rel[:, h * tq:(h + 1) * tq]
        score = score * INDEX_SCALE
        key = jnp.where(c * sc + k_pos <= t_pos, _sortable_key(score), INT_MIN)
        key_ref[c] = key
        hi_ref[c] = (key >> 16).astype(jnp.int16)
        return carry

    lax.fori_loop(0, nkc, score_body, 0)

    def count16(ref, pred):
        def body(c, acc):
            m = jnp.where(pred(ref[c]), jnp.bfloat16(1), jnp.bfloat16(0))
            for j in range(sc // COUNT_ROWS):
                acc = acc + m[j * COUNT_ROWS:(j + 1) * COUNT_ROWS]
            return acc
        acc = lax.fori_loop(0, nkc, body, jnp.zeros((COUNT_ROWS, tq), BF16))
        return jnp.sum(acc.astype(F32), axis=0, keepdims=True)

    def search16(ref, need):
        def body(i, r):
            cand = r | (jnp.int32(1) << (15 - i))
            cand16 = (cand - 32768).astype(jnp.int16)
            return jnp.where(count16(ref, lambda v: v >= cand16) >= need, cand, r)
        return lax.fori_loop(0, 16, body, jnp.zeros((1, tq), jnp.int32)) - 32768

    t_hi = search16(hi_ref, topk)
    t_hi16 = t_hi.astype(jnp.int16)
    n_hi_gt = count16(hi_ref, lambda v: v > t_hi16)

    def lo_body(c, carry):
        key = key_ref[c]
        lo_ref[c] = jnp.where((key >> 16) == t_hi, (key & 0xFFFF) - 32768, -32768).astype(jnp.int16)
        return carry

    lax.fori_loop(0, nkc, lo_body, 0)
    t_lo = search16(lo_ref, topk - n_hi_gt)
    thr = (t_hi << 16) | (t_lo + 32768)

    def count(pred_fn):
        def body(c, acc):
            m = jnp.where(pred_fn(key_ref[c], c), 1.0, 0.0)
            return acc + jnp.sum(m.reshape(sc // COUNT_ROWS, COUNT_ROWS, tq), axis=0)
        acc = lax.fori_loop(0, nkc, body, jnp.zeros((COUNT_ROWS, tq), F32))
        return jnp.sum(acc, axis=0, keepdims=True)

    n_gt = count(lambda key, c: key > thr)
    n_ge = count(lambda key, c: key >= thr)
    need = topk - n_gt
    has_thr = thr != INT_MIN
    excess = jnp.max(jnp.where(has_thr, n_ge - n_gt - need, 0.0))

    def tie_limit():
        def body(i, r):
            cand = r | (jnp.int32(1) << (15 - i))
            n = count(lambda key, c: (key == thr) & (c * sc + k_pos < cand))
            return jnp.where(n < need, cand, r)
        return lax.fori_loop(0, 16, body, jnp.zeros((1, tq), jnp.int32))

    limit = lax.cond(excess > 0.0, tie_limit, lambda: jnp.full((1, tq), 2 ** 30, jnp.int32))
    limit = jnp.where(has_thr, limit, -1)

    def mask_body(c, carry):
        key = key_ref[c]
        sel = (key > thr) | ((key == thr) & (c * sc + k_pos <= limit))
        mask = jnp.where(sel, 0.0, MASKED).astype(BF16)
        k0 = pl.multiple_of(c * sc, sc)
        for a in range(nsub):
            kx_ref[a, pl.ds(k0, sc), hd:] = mask[:, a * QSUB:(a + 1) * QSUB]
        return carry

    lax.fori_loop(0, nkc, mask_body, 0)

    for a in range(nsub):
        for h in range(DSA_HEADS):
            j = a * DSA_HEADS + h
            qx_ref[j * QSUB:(j + 1) * QSUB, :hd] = q_ref[a * QSUB:(a + 1) * QSUB, h * hd:(h + 1) * hd]

    def logits(u, c, nc):
        k0 = pl.multiple_of(c * sc, sc)
        s = lax.dot_general(qx_ref[u * rows:(u + 1) * rows, :], kx_ref[u // groups, pl.ds(k0, nc * sc), :],
                            NT_DIMS, preferred_element_type=F32)
        mx = mx_ref[...]
        for i in range(nc):
            s_ref[c + i] = s[:, i * sc:(i + 1) * sc]
        for j in range(nc * sc // LANES):
            mx = jnp.maximum(mx, s[:, j * LANES:(j + 1) * LANES])
        mx_ref[...] = mx

    def probs(c, nc):
        m = m_ref[...]
        return jnp.concatenate([jnp.exp2(s_ref[c + i, :, j * LANES:(j + 1) * LANES] - m)
                                for i in range(nc) for j in range(sc // LANES)], axis=1).astype(BF16)

    def accumulate(c, nc, p):
        acc_ref[...] += _dot(p, va_ref[pl.ds(pl.multiple_of(c * sc, sc), nc * sc), :])

    def merged(u, c, nc):
        accumulate(c, nc, probs(c, nc))
        logits(u, c, nc)

    def start_unit():
        m_ref[...] = jnp.broadcast_to(jnp.max(mx_ref[...], axis=-1, keepdims=True), m_ref.shape)
        mx_ref[...] = jnp.full(mx_ref.shape, -jnp.inf, F32)
        acc_ref[...] = jnp.zeros(acc_ref.shape, F32)

    def finish(u):
        acc = acc_ref[...]
        os_ref[u * rows:(u + 1) * rows, :] = acc[:, :hd] / acc[:, hd:hd + 1]

    def run(body):
        lax.fori_loop(0, nkc // 2, lambda i, carry: (body(2 * i, 2), carry)[1], 0)

        @pl.when(nkc % 2 == 1)
        def _():
            body(nkc - 1, 1)

    units = nsub * groups
    mx_ref[...] = jnp.full(mx_ref.shape, -jnp.inf, F32)
    run(lambda c, nc: logits(0, c, nc))
    for u in range(1, units):
        start_unit()
        run(lambda c, nc, u=u: merged(u, c, nc))
        finish(u - 1)
    start_unit()
    run(lambda c, nc: accumulate(c, nc, probs(c, nc)))
    finish(units - 1)

    for a in range(nsub):
        for h in range(DSA_HEADS):
            j = a * DSA_HEADS + h
            qr, sl = slice(a * QSUB, (a + 1) * QSUB), slice(h * hd, (h + 1) * hd)
            o_ref[qr, sl] = (os_ref[j * QSUB:(j + 1) * QSUB, :] * _silu(z_ref[qr, sl].astype(F32))).astype(BF16)


def _dsa_core(q, qi, wi, z, k, va, ki, batch, tq=256, sc=512, hg=8):
    n = q.shape[0]
    seq = n // batch
    tq, sc = min(tq, seq), min(sc, seq)
    nq = seq // tq
    topk = min(TOPK_MAX, seq // 4)
    assert seq // COUNT_ROWS <= 256 and seq <= 2 ** 16 and tq % QSUB == 0
    stacked = (tq // QSUB) * DSA_HEADS * QSUB
    blk = lambda b, i: (b * nq + i, 0)
    full = lambda b, i: (b, 0)
    return pl.pallas_call(
        functools.partial(_dsa_core_kernel, topk, tq, sc, hg),
        grid=(batch, nq),
        in_specs=[pl.BlockSpec((tq, D_INNER), blk), pl.BlockSpec((tq, IDX_HEADS * IDX_DIM), blk),
                  pl.BlockSpec((tq, LANES), blk), pl.BlockSpec((tq, D_INNER), blk),
                  pl.BlockSpec((seq, LANES), full), pl.BlockSpec((seq, 2 * LANES), full),
                  pl.BlockSpec((seq, LANES), full)],
        out_specs=pl.BlockSpec((tq, D_INNER), blk),
        out_shape=jax.ShapeDtypeStruct((n, D_INNER), BF16),
        scratch_shapes=[pltpu.VMEM((seq // sc, sc, tq), jnp.int32),
                        pltpu.VMEM((seq // sc, sc, tq), jnp.int16),
                        pltpu.VMEM((seq // sc, sc, tq), jnp.int16),
                        pltpu.VMEM((tq // QSUB, seq, 2 * LANES), BF16),
                        pltpu.VMEM((stacked, 2 * LANES), BF16),
                        pltpu.VMEM((stacked, DSA_HEAD_DIM), F32),
                        pltpu.VMEM((seq // sc, hg * QSUB, sc), F32),
                        pltpu.VMEM((hg * QSUB, LANES), F32),
                        pltpu.VMEM((hg * QSUB, LANES), F32),
                        pltpu.VMEM((hg * QSUB, 2 * LANES), F32)],
        compiler_params=_params("parallel", "arbitrary"),
        name="dsa_core",
    )(q, qi, wi, z, k, va, ki)


def _dsa_mixer(x, pos, w_in, batch):
    w = w_in.astype(BF16)
    hd = DSA_HEAD_DIM
    o1 = D_INNER
    o3 = o1 + 2 * hd
    o4 = o3 + D_INNER
    o5 = o4 + IDX_HEADS * IDX_DIM
    o6 = o5 + IDX_DIM
    q, k, va, qi, ki, wi = _dsa_prep(x, pos, w[:, :o1], w[:, o1:o3], w[:, o4:o5], w[:, o5:o6], w[:, o6:])
    z = _mm(x, w[:, o3:o4], BF16)
    return _dsa_core(q, qi, wi, z, k, va, ki, batch)


CONV_HALO = 32
CONV_ROWS = 64
CONV_COLS = 256
SUBLANES = 8
NORM_ROWS = 128


def _conv_core_kernel(u_ref, z_ref, w_ref, b_ref, lng_ref, lnb_ref, o_ref,
                      buf_ref, sh_ref, conv_ref):
    t, ch = u_ref.shape

    @pl.when(pl.program_id(1) == 0)
    def _():
        buf_ref[0:CONV_HALO, :] = jnp.zeros((CONV_HALO, ch), F32)

    buf_ref[CONV_HALO:CONV_HALO + t, :] = u_ref[...]
    first = CONV_HALO - (CONV_WIDTH - 1)
    span = t + CONV_HALO - SUBLANES
    for cb in range(ch // CONV_COLS):
        cols = slice(cb * CONV_COLS, (cb + 1) * CONV_COLS)
        for s in range(1, SUBLANES):
            sh_ref[s - 1] = buf_ref[s:s + span, cols]

        def tile(i, carry, cols=cols):
            r0 = pl.multiple_of(i * CONV_ROWS, CONV_ROWS)
            acc = jnp.zeros((CONV_ROWS, CONV_COLS), F32)
            for j in range(CONV_WIDTH):
                s = (first + j) % SUBLANES
                base = r0 + (first + j - s)
                if s == 0:
                    src = buf_ref[pl.ds(base, CONV_ROWS), cols]
                else:
                    src = sh_ref[s - 1, pl.ds(base, CONV_ROWS), :]
                acc = acc + src * w_ref[j:j + 1, cols]
            conv_ref[pl.ds(r0, CONV_ROWS), cols] = acc + b_ref[:, cols]
            return carry

        lax.fori_loop(0, t // CONV_ROWS, tile, 0)

    def norm_tile(i, carry):
        rows = pl.ds(pl.multiple_of(i * NORM_ROWS, NORM_ROWS), NORM_ROWS)
        u = conv_ref[rows, :]
        mu = jnp.mean(u, axis=-1, keepdims=True)
        d = u - mu
        var = jnp.mean(d * d, axis=-1, keepdims=True)
        u = _silu(d * lax.rsqrt(var + NORM_EPS) * lng_ref[...] + lnb_ref[...])
        o_ref[rows, :] = (u * _silu(z_ref[rows, :].astype(F32))).astype(BF16)
        return carry

    lax.fori_loop(0, t // NORM_ROWS, norm_tile, 0)
    buf_ref[0:CONV_HALO, :] = buf_ref[t:t + CONV_HALO, :]


def _conv_core(u, z, dw_w, dw_b, ln_g, ln_b, batch, t=256):
    n, c = u.shape
    seq = n // batch
    t = min(t, seq)
    nt = seq // t
    blk = lambda b, i: (b * nt + i, 0)
    whole = lambda b, i: (0, 0)
    w_p = jnp.zeros((CONV_HALO, c), F32).at[:CONV_WIDTH].set(dw_w)
    vec = pl.BlockSpec((1, c), whole)
    return pl.pallas_call(
        _conv_core_kernel,
        grid=(batch, nt),
        in_specs=[pl.BlockSpec((t, c), blk)] * 2 + [pl.BlockSpec(w_p.shape, whole), vec, vec, vec],
        out_specs=pl.BlockSpec((t, c), blk),
        out_shape=jax.ShapeDtypeStruct((n, c), BF16),
        scratch_shapes=[pltpu.VMEM((t + CONV_HALO, c), F32),
                        pltpu.VMEM((SUBLANES - 1, t + CONV_HALO - SUBLANES, CONV_COLS), F32),
                        pltpu.VMEM((t, c), F32)],
        compiler_params=_params("parallel", "arbitrary"),
        name="conv_core",
    )(u, z, w_p, dw_b.reshape(1, c), ln_g.reshape(1, c), ln_b.reshape(1, c))


def _conv_mixer(x, w_in, dw_w, dw_b, ln_g, ln_b, batch):
    w = w_in.astype(BF16)
    u = _mm_glu(x, w, D_INNER)
    z = _mm(x, w, BF16, col0=2 * D_INNER)
    return _conv_core(u, z, dw_w, dw_b, ln_g, ln_b, batch)


def kernel(x, p, positions, gla_w_in, gla_w_a2, gla_b_a, gla_gn_g, gla_w_out, dsa_w_in, dsa_w_out,
           conv_w_in, conv_dw_w, conv_dw_b, conv_ln_g, conv_ln_b, conv_w_out, ln_g, ln_b, ple_w,
           ple_gate_w):
    batch, seq, d = x.shape
    depth = p.shape[0]
    n = batch * seq
    alpha = (2 * depth) ** 0.25
    xf = x.reshape(n, d)
    xb = xf
    pos = positions.astype(F32).reshape(n, 1)
    p_all = p.reshape(depth * n, p.shape[-1])
    for i in range(depth):
        kind, j = i % N_MIXERS, i // N_MIXERS
        if kind == 0:
            u = _gla_mixer(xb, gla_w_in[j], gla_w_a2[j], gla_b_a[j], gla_gn_g[j], batch)
            w_out = gla_w_out[j]
        elif kind == 1:
            u = _dsa_mixer(xb, pos, dsa_w_in[j], batch)
            w_out = dsa_w_out[j]
        else:
            u = _conv_mixer(xb, conv_w_in[j], conv_dw_w[j], conv_dw_b[j], conv_ln_g[j], conv_ln_b[j], batch)
            w_out = conv_w_out[j]
        xf, xb = _post(u, xf, p_all, i, w_out, ln_g[i], ln_b[i], ple_w[i], ple_gate_w[i], alpha)
    return xf.reshape(batch, seq, d)
```

```python
import functools

import jax
import jax.numpy as jnp
from jax import lax
from jax.experimental import pallas as pl
from jax.experimental.pallas import tpu as pltpu

F32 = jnp.float32
BF16 = jnp.bfloat16

D_MODEL = 1024
N_MIXERS = 3
D_INNER = 2 * D_MODEL
GLA_HEADS = 4
GLA_DK = D_MODEL // GLA_HEADS
GLA_DV = D_INNER // GLA_HEADS
GLA_RANK = 16
GLA_GATE_NORM = 16.0
GLA_CHUNK = 64
DSA_HEADS = 16
DSA_HEAD_DIM = D_INNER // DSA_HEADS
IDX_HEADS = 8
IDX_DIM = 64
TOPK_MAX = 256
ATTN_SCALE = DSA_HEAD_DIM ** -0.5
INDEX_SCALE = (IDX_HEADS ** -0.5) * (IDX_DIM ** -0.5)
CONV_WIDTH = 31
ROPE_THETA = 500000.0
DSA_ROT_DIM = DSA_HEAD_DIM // 4
IDX_ROT_DIM = IDX_DIM // 4
PLE_DIM = 256
NORM_EPS = 1e-5

LANES = 128
LOG2E = 1.4426950408889634
INT_MIN = -(2 ** 31)
VMEM_LIMIT = 56 * 1024 * 1024

NT_DIMS = (((1,), (1,)), ((), ()))
TN_DIMS = (((0,), (0,)), ((), ()))


def _params(*sem):
    return pltpu.CompilerParams(dimension_semantics=sem, vmem_limit_bytes=VMEM_LIMIT)


def _dot(a, b):
    return jnp.dot(a, b, preferred_element_type=F32)


def _silu(x):
    return x * jax.nn.sigmoid(x)


def _mm_kernel(x_ref, w_ref, o_ref):
    o_ref[...] = _dot(x_ref[...].astype(BF16), w_ref[...]).astype(o_ref.dtype)


def _mm(x, w, out_dtype, col0=0, cols=None, tm=1024, tn=1024):
    n, k = x.shape
    c = w.shape[1] - col0 if cols is None else cols
    tm, tn = min(tm, n), min(tn, c)
    assert col0 % tn == 0 and c % tn == 0
    j0 = col0 // tn
    return pl.pallas_call(
        _mm_kernel,
        grid=(n // tm, c // tn),
        in_specs=[pl.BlockSpec((tm, k), lambda i, j: (i, 0)),
                  pl.BlockSpec((k, tn), lambda i, j: (0, j0 + j))],
        out_specs=pl.BlockSpec((tm, tn), lambda i, j: (i, j)),
        out_shape=jax.ShapeDtypeStruct((n, c), out_dtype),
        compiler_params=_params("parallel", "parallel"),
        name="proj",
    )(x, w)


def _mm_glu_kernel(x_ref, wa_ref, wg_ref, o_ref):
    xb = x_ref[...].astype(BF16)
    o_ref[...] = _dot(xb, wa_ref[...]) * jax.nn.sigmoid(_dot(xb, wg_ref[...]))


def _mm_glu(x, w, c, tm=1024, tn=512):
    n, k = x.shape
    tm, tn = min(tm, n), min(tn, c)
    assert c % tn == 0
    nj = c // tn
    return pl.pallas_call(
        _mm_glu_kernel,
        grid=(n // tm, nj),
        in_specs=[pl.BlockSpec((tm, k), lambda i, j: (i, 0)),
                  pl.BlockSpec((k, tn), lambda i, j: (0, j)),
                  pl.BlockSpec((k, tn), lambda i, j: (0, nj + j))],
        out_specs=pl.BlockSpec((tm, tn), lambda i, j: (i, j)),
        out_shape=jax.ShapeDtypeStruct((n, c), F32),
        compiler_params=_params("parallel", "parallel"),
        name="proj_glu",
    )(x, w, w)


def _post_kernel(alpha, u_ref, x_ref, p_ref, wout_ref, lng_ref, lnb_ref, plew_ref, gatew_ref, o_ref, ob_ref):
    y = _dot(u_ref[...], wout_ref[...])
    ple = _dot(p_ref[...].astype(BF16), plew_ref[...])
    r = alpha * x_ref[...] + y
    mu = jnp.mean(r, axis=-1, keepdims=True)
    d = r - mu
    var = jnp.mean(d * d, axis=-1, keepdims=True)
    x1 = d * lax.rsqrt(var + NORM_EPS) * lng_ref[...] + lnb_ref[...]
    gate = _dot(x1.astype(BF16), gatew_ref[...])
    out = x1 + ple * jax.nn.sigmoid(gate)
    o_ref[...] = out
    ob_ref[...] = out.astype(BF16)


def _post(u, x, p_all, layer, w_out, ln_g, ln_b, ple_w, gate_w, alpha, tm=512):
    n, d = x.shape
    tm = min(tm, n)
    row = lambda i: (i, 0)
    whole = lambda i: (0, 0)
    p0 = layer * (n // tm)
    return pl.pallas_call(
        functools.partial(_post_kernel, alpha),
        grid=(n // tm,),
        in_specs=[pl.BlockSpec((tm, u.shape[1]), row),
                  pl.BlockSpec((tm, d), row),
                  pl.BlockSpec((tm, p_all.shape[1]), lambda i: (p0 + i, 0)),
                  pl.BlockSpec(w_out.shape, whole),
                  pl.BlockSpec((1, d), whole),
                  pl.BlockSpec((1, d), whole),
                  pl.BlockSpec(ple_w.shape, whole),
                  pl.BlockSpec(gate_w.shape, whole)],
        out_specs=[pl.BlockSpec((tm, d), row)] * 2,
        out_shape=[jax.ShapeDtypeStruct((n, d), F32), jax.ShapeDtypeStruct((n, d), BF16)],
        compiler_params=_params("parallel"),
        name="post",
    )(u, x, p_all, w_out.astype(BF16), ln_g.reshape(1, d), ln_b.reshape(1, d),
      ple_w.astype(BF16), gate_w.astype(BF16))


def _split3_bf16(x):
    hi = x.astype(BF16)
    r = x - hi.astype(F32)
    mid = r.astype(BF16)
    lo = (r - mid.astype(F32)).astype(BF16)
    return hi, mid, lo


def _gla_gate_kernel(x_ref, wq_ref, wk_ref, wa_ref, wa2_ref, ba_ref,
                     qt_ref, kt_ref, ks_ref, dec_ref):
    xb = x_ref[...].astype(BF16)
    a = _dot(xb, wa_ref[...])
    pre = _dot(a.astype(BF16), wa2_ref[...]) + ba_ref[...]
    g = (jnp.minimum(pre, 0.0) - jnp.log(1.0 + jnp.exp(-jnp.abs(pre)))) / GLA_GATE_NORM
    ch = GLA_CHUNK
    ri = lax.broadcasted_iota(jnp.int32, (ch, ch), 0)
    ci = lax.broadcasted_iota(jnp.int32, (ch, ch), 1)
    tri = jnp.where(ri >= ci, 1.0, 0.0).astype(BF16)
    chunks = [slice(c * ch, (c + 1) * ch) for c in range(x_ref.shape[0] // ch)]
    cums = []
    for sl in chunks:
        hi, mid, lo = _split3_bf16(g[sl])
        cums.append(_dot(tri, hi) + _dot(tri, mid) + _dot(tri, lo))
    q = _dot(xb, wq_ref[...])
    for sl, cum in zip(chunks, cums):
        qt_ref[sl, :] = (q[sl] * (GLA_DK ** -0.5) * jnp.exp(cum)).astype(BF16)
    k = _dot(xb, wk_ref[...])
    for sl, cum in zip(chunks, cums):
        last = cum[ch - 1:ch]
        kt_ref[sl, :] = (k[sl] * jnp.exp(-cum)).astype(BF16)
        ks_ref[sl, :] = (k[sl] * jnp.exp(last - cum)).astype(BF16)
    dec_ref[...] = jnp.concatenate([jnp.exp(cum[ch - 1:ch]) for cum in cums], axis=0)


def _gla_gate(x, w, wa, wa2, ba, tm=512):
    n, d = x.shape
    qk = GLA_HEADS * GLA_DK
    row = lambda i: (i, 0)
    whole = lambda i: (0, 0)
    wa_p = jnp.zeros((d, LANES), BF16).at[:, :GLA_RANK].set(wa)
    wa2_p = jnp.zeros((LANES, qk), BF16).at[:GLA_RANK].set(wa2)
    act = jax.ShapeDtypeStruct((n, qk), BF16)
    return pl.pallas_call(
        _gla_gate_kernel,
        grid=(n // tm,),
        in_specs=[pl.BlockSpec((tm, d), row),
                  pl.BlockSpec((d, qk), lambda i: (0, 0)), pl.BlockSpec((d, qk), lambda i: (0, 1)),
                  pl.BlockSpec(wa_p.shape, whole), pl.BlockSpec(wa2_p.shape, whole),
                  pl.BlockSpec((1, qk), whole)],
        out_specs=[pl.BlockSpec((tm, qk), row)] * 3 + [pl.BlockSpec((tm // GLA_CHUNK, qk), row)],
        out_shape=[act, act, act, jax.ShapeDtypeStruct((n // GLA_CHUNK, qk), F32)],
        compiler_params=_params("parallel"),
        name="gla_gate",
    )(x, w, w, wa_p, wa2_p, ba.reshape(1, qk))


def _gla_core_kernel(qt_ref, kt_ref, ks_ref, v_ref, dec_ref, z_ref, gn_ref, o_ref, s_ref, sb_ref, oi_ref):
    @pl.when(pl.program_id(2) == 0)
    def _():
        s_ref[...] = jnp.zeros_like(s_ref)
        sb_ref[...] = jnp.zeros_like(sb_ref)

    ch, dk, dv = GLA_CHUNK, GLA_DK, GLA_DV
    ri = lax.broadcasted_iota(jnp.int32, (ch, ch), 0)
    ci = lax.broadcasted_iota(jnp.int32, (ch, ch), 1)
    tril = ri >= ci
    dec_t = [dec_ref[:, j * dk:(j + 1) * dk].T for j in range(GLA_HEADS_PER_STEP)]
    chunks = [(c, j) for c in range(qt_ref.shape[0] // ch) for j in range(GLA_HEADS_PER_STEP)]

    def tiles(c, j):
        return slice(c * ch, (c + 1) * ch), slice(j * dk, (j + 1) * dk), slice(j * dv, (j + 1) * dv)

    a = {}
    for c, j in chunks:
        sl, kc, _ = tiles(c, j)
        a[c, j] = jnp.where(tril, lax.dot_general(qt_ref[sl, kc], kt_ref[sl, kc], NT_DIMS,
                                                  preferred_element_type=F32), 0.0).astype(BF16)
    for c, j in chunks:
        sl, _, vc = tiles(c, j)
        oi_ref[sl, vc] = _dot(a[c, j], v_ref[sl, vc])
    for c, j in chunks:
        sl, kc, vc = tiles(c, j)
        v = v_ref[sl, vc]
        o = oi_ref[sl, vc] + _dot(qt_ref[sl, kc], sb_ref[j])
        s = s_ref[j] * dec_t[j][:, c:c + 1] + lax.dot_general(ks_ref[sl, kc], v, TN_DIMS, preferred_element_type=F32)
        s_ref[j] = s
        sb_ref[j] = s.astype(BF16)
        o = o * lax.rsqrt(jnp.mean(o * o, axis=-1, keepdims=True) + NORM_EPS)
        o = o * gn_ref[:, vc]
        o_ref[sl, vc] = (o * _silu(z_ref[sl, vc].astype(F32))).astype(BF16)


GLA_HEADS_PER_STEP = 2


def _gla_core(qt, kt, ks, vz, dec, gn_g, batch, tc=1024):
    n = qt.shape[0]
    seq = n // batch
    tc = min(tc, seq)
    nt = seq // tc
    hb = GLA_HEADS_PER_STEP
    groups = GLA_HEADS // hb
    tok = lambda b, g, i: (b * nt + i, g)
    return pl.pallas_call(
        _gla_core_kernel,
        grid=(batch, groups, nt),
        in_specs=[pl.BlockSpec((tc, hb * GLA_DK), tok)] * 3
                 + [pl.BlockSpec((tc, hb * GLA_DV), tok),
                    pl.BlockSpec((tc // GLA_CHUNK, hb * GLA_DK), tok),
                    pl.BlockSpec((tc, hb * GLA_DV), lambda b, g, i: (b * nt + i, groups + g)),
                    pl.BlockSpec((1, hb * GLA_DV), lambda b, g, i: (0, g))],
        out_specs=pl.BlockSpec((tc, hb * GLA_DV), tok),
        out_shape=jax.ShapeDtypeStruct((n, D_INNER), BF16),
        scratch_shapes=[pltpu.VMEM((hb, GLA_DK, GLA_DV), F32), pltpu.VMEM((hb, GLA_DK, GLA_DV), BF16),
                        pltpu.VMEM((tc, hb * GLA_DV), F32)],
        compiler_params=_params("parallel", "parallel", "arbitrary"),
        name="gla_core",
    )(qt, kt, ks, vz, dec, vz, gn_g.reshape(1, D_INNER))


def _gla_mixer(x, w_in, w_a2, b_a, gn_g, batch):
    qk = GLA_HEADS * GLA_DK
    w = w_in.astype(BF16)
    o_v, o_a = 2 * qk, 2 * qk + 2 * D_INNER
    qt, kt, ks, dec = _gla_gate(x, w, w[:, o_a:], w_a2.astype(BF16), b_a)
    vz = _mm(x, w, BF16, col0=o_v, cols=o_a - o_v)
    return _gla_core(qt, kt, ks, vz, dec, gn_g, batch)


def _rope(x, cos, sin_lo, sin_hi, half):
    return (x * cos + pltpu.roll(x, half, 1) * sin_hi
            + pltpu.roll(x, LANES - half, 1) * sin_lo)


def _rope_tables(pos, invf, half, period):
    ang = pos * invf
    cos, sin = jnp.cos(ang), jnp.sin(ang)
    lane = lax.broadcasted_iota(jnp.int32, ang.shape, 1) % period
    sin_lo = jnp.where(lane < half, -sin, 0.0)
    sin_hi = jnp.where((lane >= half) & (lane < 2 * half), sin, 0.0)
    return cos, sin_lo, sin_hi


def _dsa_prep_kernel(x_ref, pos_ref, invh_ref, invi_ref, wq_ref, wkv_ref, wqi_ref, wki_ref, wwi_ref,
                     q_ref, k_ref, v_ref, qi_ref, ki_ref, wi_ref):
    xb = x_ref[...].astype(BF16)
    pos = pos_ref[...]
    hd = DSA_HEAD_DIM
    cos_h, slo_h, shi_h = _rope_tables(pos, invh_ref[...], DSA_ROT_DIM // 2, hd)
    cos_i, slo_i, shi_i = _rope_tables(pos, invi_ref[...], IDX_ROT_DIM // 2, IDX_DIM)
    q = _dot(xb, wq_ref[...])
    for h in range(DSA_HEADS):
        sl = slice(h * hd, (h + 1) * hd)
        q_ref[:, sl] = (_rope(q[:, sl], cos_h, slo_h, shi_h, DSA_ROT_DIM // 2)
                        * (ATTN_SCALE * LOG2E)).astype(BF16)
    kv = _dot(xb, wkv_ref[...])
    k_ref[...] = _rope(kv[:, :hd], cos_h, slo_h, shi_h, DSA_ROT_DIM // 2).astype(BF16)
    v_ref[:, :hd] = kv[:, hd:].astype(BF16)
    v_ref[:, hd:] = jnp.ones((x_ref.shape[0], LANES), BF16)
    qi = _dot(xb, wqi_ref[...])
    for j in range(qi.shape[1] // LANES):
        sl = slice(j * LANES, (j + 1) * LANES)
        qi_ref[:, sl] = _rope(qi[:, sl], cos_i, slo_i, shi_i, IDX_ROT_DIM // 2).astype(BF16)
    ki = _dot(xb, wki_ref[...])
    ki_ref[...] = _rope(ki, cos_i, slo_i, shi_i, IDX_ROT_DIM // 2).astype(BF16)
    wi_ref[...] = _dot(xb, wwi_ref[...])


def _lane_invfreq(rot_dim, period):
    inv = ROPE_THETA ** (-jnp.arange(0, rot_dim, 2, dtype=F32) / rot_dim)
    head = jnp.concatenate([inv, inv, jnp.zeros((period - rot_dim,), F32)])
    return jnp.tile(head, LANES // period).reshape(1, LANES)


def _pad_cols(w, width):
    return jnp.zeros((w.shape[0], width), w.dtype).at[:, :w.shape[1]].set(w)


def _dsa_prep(x, pos, wq, wkv, wqi, wki, wwi, tm=512):
    n, d = x.shape
    tm = min(tm, n)
    row = lambda i: (i, 0)
    whole = lambda i: (0, 0)
    wki_p, wwi_p = _pad_cols(wki, LANES), _pad_cols(wwi, LANES)
    ws = [wq, wkv, wqi, wki_p, wwi_p]
    return pl.pallas_call(
        _dsa_prep_kernel,
        grid=(n // tm,),
        in_specs=[pl.BlockSpec((tm, d), row), pl.BlockSpec((tm, 1), row),
                  pl.BlockSpec((1, LANES), whole), pl.BlockSpec((1, LANES), whole)]
                 + [pl.BlockSpec(w.shape, whole) for w in ws],
        out_specs=[pl.BlockSpec((tm, D_INNER), row), pl.BlockSpec((tm, LANES), row),
                   pl.BlockSpec((tm, 2 * LANES), row), pl.BlockSpec((tm, IDX_HEADS * IDX_DIM), row),
                   pl.BlockSpec((tm, LANES), row), pl.BlockSpec((tm, LANES), row)],
        out_shape=[jax.ShapeDtypeStruct((n, D_INNER), BF16), jax.ShapeDtypeStruct((n, LANES), BF16),
                   jax.ShapeDtypeStruct((n, 2 * LANES), BF16),
                   jax.ShapeDtypeStruct((n, IDX_HEADS * IDX_DIM), BF16),
                   jax.ShapeDtypeStruct((n, LANES), BF16), jax.ShapeDtypeStruct((n, LANES), F32)],
        compiler_params=_params("parallel"),
        name="dsa_prep",
    )(x, pos, _lane_invfreq(DSA_ROT_DIM, DSA_HEAD_DIM), _lane_invfreq(IDX_ROT_DIM, IDX_DIM), *ws)


COUNT_ROWS = 64
MASKED = -1e30
QSUB = 128


def _sortable_key(score):
    b = pltpu.bitcast(score, jnp.int32)
    key = b ^ ((b >> 31) & 0x7FFFFFFF)
    return jnp.where(score == 0.0, 0, key)


def _dsa_core_kernel(topk, tq, sc, hg,
                     q_ref, qi_ref, wi_ref, z_ref, k_ref, va_ref, ki_ref, o_ref,
                     key_ref, hi_ref, lo_ref, kx_ref, qx_ref, os_ref, s_ref, mx_ref, m_ref, acc_ref):
    qb = pl.program_id(1)
    t0 = qb * tq
    nkc = (t0 + tq + sc - 1) // sc
    hd = DSA_HEAD_DIM
    nsub = tq // QSUB
    groups = DSA_HEADS // hg
    rows = hg * QSUB

    @pl.when(qb == 0)
    def _():
        for a in range(nsub):
            kx_ref[a, :, :hd] = k_ref[...]
        ri = lax.broadcasted_iota(jnp.int32, (QSUB, LANES), 0)
        ci = lax.broadcasted_iota(jnp.int32, (QSUB, LANES), 1)
        eye = jnp.where(ri == ci, 1.0, 0.0).astype(BF16)
        for j in range(nsub * DSA_HEADS):
            qx_ref[j * QSUB:(j + 1) * QSUB, hd:] = eye

    qi_s = jnp.concatenate([qi_ref[:, h * IDX_DIM:(h + 1) * IDX_DIM] for h in range(IDX_HEADS)], axis=0)
    wi_t = wi_ref[...].T
    wi_row = jnp.concatenate([wi_t[h:h + 1, :] for h in range(IDX_HEADS)], axis=1)
    k_pos = lax.broadcasted_iota(jnp.int32, (sc, tq), 0)
    t_pos = t0 + lax.broadcasted_iota(jnp.int32, (sc, tq), 1)

    def run(body):
        lax.fori_loop(0, nkc // 2, lambda i, carry: (body(2 * i, 2), carry)[1], 0)

        @pl.when(nkc % 2 == 1)
        def _():
            body(nkc - 1, 1)

    def score_chunks(c, nc):
        ki_c = ki_ref[pl.ds(pl.multiple_of(c * sc, sc), nc * sc), :][:, :IDX_DIM]
        rel_all = lax.dot_general(ki_c, qi_s, NT_DIMS, preferred_element_type=F32)
        for i in range(nc):
            rel = jnp.maximum(rel_all[i * sc:(i + 1) * sc], 0.0) * wi_row
            score = rel[:, :tq]
            for h in range(1, IDX_HEADS):
                score = score + rel[:, h * tq:(h + 1) * tq]
            score = score * INDEX_SCALE
            key = jnp.where((c + i) * sc + k_pos <= t_pos, _sortable_key(score), INT_MIN)
            key_ref[c + i] = key
            hi_ref[c + i] = (key >> 16).astype(jnp.int16)

    run(score_chunks)

    def count16(ref, pred):
        def body(c, acc):
            m = jnp.where(pred(ref[c]), jnp.bfloat16(1), jnp.bfloat16(0))
            for j in range(sc // COUNT_ROWS):
                acc = acc + m[j * COUNT_ROWS:(j + 1) * COUNT_ROWS]
            return acc
        acc = lax.fori_loop(0, nkc, body, jnp.zeros((COUNT_ROWS, tq), BF16))
        return jnp.sum(acc.astype(F32), axis=0, keepdims=True)

    def search16(ref, need):
        def body(i, r):
            cand = r | (jnp.int32(1) << (15 - i))
            cand16 = (cand - 32768).astype(jnp.int16)
            return jnp.where(count16(ref, lambda v: v >= cand16) >= need, cand, r)
        return lax.fori_loop(0, 16, body, jnp.zeros((1, tq), jnp.int32)) - 32768

    t_hi = search16(hi_ref, topk)
    t_hi16 = t_hi.astype(jnp.int16)
    n_hi_gt = count16(hi_ref, lambda v: v > t_hi16)

    def lo_body(c, carry):
        key = key_ref[c]
        lo_ref[c] = jnp.where((key >> 16) == t_hi, (key & 0xFFFF) - 32768, -32768).astype(jnp.int16)
        return carry

    lax.fori_loop(0, nkc, lo_body, 0)
    t_lo = search16(lo_ref, topk - n_hi_gt)
    thr = (t_hi << 16) | (t_lo + 32768)

    def count(pred_fn):
        def body(c, acc):
            m = jnp.where(pred_fn(key_ref[c], c), 1.0, 0.0)
            return acc + jnp.sum(m.reshape(sc // COUNT_ROWS, COUNT_ROWS, tq), axis=0)
        acc = lax.fori_loop(0, nkc, body, jnp.zeros((COUNT_ROWS, tq), F32))
        return jnp.sum(acc, axis=0, keepdims=True)

    n_gt = count(lambda key, c: key > thr)
    n_ge = count(lambda key, c: key >= thr)
    need = topk - n_gt
    has_thr = thr != INT_MIN
    excess = jnp.max(jnp.where(has_thr, n_ge - n_gt - need, 0.0))

    def tie_limit():
        def body(i, r):
            cand = r | (jnp.int32(1) << (15 - i))
            n = count(lambda key, c: (key == thr) & (c * sc + k_pos < cand))
            return jnp.where(n < need, cand, r)
        return lax.fori_loop(0, 16, body, jnp.zeros((1, tq), jnp.int32))

    limit = lax.cond(excess > 0.0, tie_limit, lambda: jnp.full((1, tq), 2 ** 30, jnp.int32))
    limit = jnp.where(has_thr, limit, -1)

    def mask_body(c, carry):
        key = key_ref[c]
        sel = (key > thr) | ((key == thr) & (c * sc + k_pos <= limit))
        mask = jnp.where(sel, 0.0, MASKED).astype(BF16)
        k0 = pl.multiple_of(c * sc, sc)
        for a in range(nsub):
            kx_ref[a, pl.ds(k0, sc), hd:] = mask[:, a * QSUB:(a + 1) * QSUB]
        return carry

    lax.fori_loop(0, nkc, mask_body, 0)

    for a in range(nsub):
        for h in range(DSA_HEADS):
            j = a * DSA_HEADS + h
            qx_ref[j * QSUB:(j + 1) * QSUB, :hd] = q_ref[a * QSUB:(a + 1) * QSUB, h * hd:(h + 1) * hd]

    def logits(u, c, nc):
        k0 = pl.multiple_of(c * sc, sc)
        s = lax.dot_general(qx_ref[u * rows:(u + 1) * rows, :], kx_ref[u // groups, pl.ds(k0, nc * sc), :],
                            NT_DIMS, preferred_element_type=F32)
        mx = mx_ref[...]
        for i in range(nc):
            s_ref[c + i] = s[:, i * sc:(i + 1) * sc]
        for j in range(nc * sc // LANES):
            mx = jnp.maximum(mx, s[:, j * LANES:(j + 1) * LANES])
        mx_ref[...] = mx

    def probs(c, nc):
        m = m_ref[...]
        return jnp.concatenate([jnp.exp2(s_ref[c + i, :, j * LANES:(j + 1) * LANES] - m)
                                for i in range(nc) for j in range(sc // LANES)], axis=1).astype(BF16)

    def accumulate(c, nc, p):
        acc_ref[...] += _dot(p, va_ref[pl.ds(pl.multiple_of(c * sc, sc), nc * sc), :])

    def merged(u, c, nc):
        accumulate(c, nc, probs(c, nc))
        logits(u, c, nc)

    def start_unit():
        m_ref[...] = jnp.broadcast_to(jnp.max(mx_ref[...], axis=-1, keepdims=True), m_ref.shape)
        mx_ref[...] = jnp.full(mx_ref.shape, -jnp.inf, F32)
        acc_ref[...] = jnp.zeros(acc_ref.shape, F32)

    def finish(u):
        acc = acc_ref[...]
        os_ref[u * rows:(u + 1) * rows, :] = acc[:, :hd] / acc[:, hd:]

    units = nsub * groups
    mx_ref[...] = jnp.full(mx_ref.shape, -jnp.inf, F32)
    run(lambda c, nc: logits(0, c, nc))
    for u in range(1, units):
        start_unit()
        run(lambda c, nc, u=u: merged(u, c, nc))
        finish(u - 1)
    start_unit()
    run(lambda c, nc: accumulate(c, nc, probs(c, nc)))
    finish(units - 1)

    for a in range(nsub):
        for h in range(DSA_HEADS):
            j = a * DSA_HEADS + h
            qr, sl = slice(a * QSUB, (a + 1) * QSUB), slice(h * hd, (h + 1) * hd)
            o_ref[qr, sl] = (os_ref[j * QSUB:(j + 1) * QSUB, :] * _silu(z_ref[qr, sl].astype(F32))).astype(BF16)


def _dsa_core(q, qi, wi, z, k, va, ki, batch, tq=256, sc=512, hg=8):
    n = q.shape[0]
    seq = n // batch
    tq, sc = min(tq, seq), min(sc, seq)
    nq = seq // tq
    topk = min(TOPK_MAX, seq // 4)
    assert seq // COUNT_ROWS <= 256 and seq <= 2 ** 16 and tq % QSUB == 0
    stacked = (tq // QSUB) * DSA_HEADS * QSUB
    blk = lambda b, i: (b * nq + i, 0)
    full = lambda b, i: (b, 0)
    return pl.pallas_call(
        functools.partial(_dsa_core_kernel, topk, tq, sc, hg),
        grid=(batch, nq),
        in_specs=[pl.BlockSpec((tq, D_INNER), blk), pl.BlockSpec((tq, IDX_HEADS * IDX_DIM), blk),
                  pl.BlockSpec((tq, LANES), blk), pl.BlockSpec((tq, D_INNER), blk),
                  pl.BlockSpec((seq, LANES), full), pl.BlockSpec((seq, 2 * LANES), full),
                  pl.BlockSpec((seq, LANES), full)],
        out_specs=pl.BlockSpec((tq, D_INNER), blk),
        out_shape=jax.ShapeDtypeStruct((n, D_INNER), BF16),
        scratch_shapes=[pltpu.VMEM((seq // sc, sc, tq), jnp.int32),
                        pltpu.VMEM((seq // sc, sc, tq), jnp.int16),
                        pltpu.VMEM((seq // sc, sc, tq), jnp.int16),
                        pltpu.VMEM((tq // QSUB, seq, 2 * LANES), BF16),
                        pltpu.VMEM((stacked, 2 * LANES), BF16),
                        pltpu.VMEM((stacked, DSA_HEAD_DIM), F32),
                        pltpu.VMEM((seq // sc, hg * QSUB, sc), F32),
                        pltpu.VMEM((hg * QSUB, LANES), F32),
                        pltpu.VMEM((hg * QSUB, LANES), F32),
                        pltpu.VMEM((hg * QSUB, 2 * LANES), F32)],
        compiler_params=_params("parallel", "arbitrary"),
        name="dsa_core",
    )(q, qi, wi, z, k, va, ki)


def _dsa_mixer(x, pos, w_in, batch):
    w = w_in.astype(BF16)
    hd = DSA_HEAD_DIM
    o1 = D_INNER
    o3 = o1 + 2 * hd
    o4 = o3 + D_INNER
    o5 = o4 + IDX_HEADS * IDX_DIM
    o6 = o5 + IDX_DIM
    q, k, va, qi, ki, wi = _dsa_prep(x, pos, w[:, :o1], w[:, o1:o3], w[:, o4:o5], w[:, o5:o6], w[:, o6:])
    z = _mm(x, w[:, o3:o4], BF16)
    return _dsa_core(q, qi, wi, z, k, va, ki, batch)


CONV_HALO = 32
CONV_ROWS = 128
CONV_COLS = 128
SUBLANES = 8
NORM_ROWS = 128


def _conv_core_kernel(u_ref, z_ref, w_ref, b_ref, lng_ref, lnb_ref, o_ref,
                      buf_ref, sh_ref, conv_ref):
    t, ch = u_ref.shape

    @pl.when(pl.program_id(1) == 0)
    def _():
        buf_ref[0:CONV_HALO, :] = jnp.zeros((CONV_HALO, ch), F32)

    buf_ref[CONV_HALO:CONV_HALO + t, :] = u_ref[...]
    first = CONV_HALO - (CONV_WIDTH - 1)
    span = t + CONV_HALO - SUBLANES
    for cb in range(ch // CONV_COLS):
        cols = slice(cb * CONV_COLS, (cb + 1) * CONV_COLS)
        for s in range(1, SUBLANES):
            sh_ref[s - 1] = buf_ref[s:s + span, cols]

        def tile(i, carry, cols=cols):
            r0 = pl.multiple_of(i * CONV_ROWS, CONV_ROWS)
            acc = jnp.zeros((CONV_ROWS, CONV_COLS), F32)
            for j in range(CONV_WIDTH):
                s = (first + j) % SUBLANES
                base = r0 + (first + j - s)
                if s == 0:
                    src = buf_ref[pl.ds(base, CONV_ROWS), cols]
                else:
                    src = sh_ref[s - 1, pl.ds(base, CONV_ROWS), :]
                acc = acc + src * w_ref[j:j + 1, cols]
            conv_ref[pl.ds(r0, CONV_ROWS), cols] = acc + b_ref[:, cols]
            return carry

        lax.fori_loop(0, t // CONV_ROWS, tile, 0)

    def norm_tile(i, carry):
        rows = pl.ds(pl.multiple_of(i * NORM_ROWS, NORM_ROWS), NORM_ROWS)
        u = conv_ref[rows, :]
        mu = jnp.mean(u, axis=-1, keepdims=True)
        d = u - mu
        var = jnp.mean(d * d, axis=-1, keepdims=True)
        u = _silu(d * lax.rsqrt(var + NORM_EPS) * lng_ref[...] + lnb_ref[...])
        o_ref[rows, :] = (u * _silu(z_ref[rows, :].astype(F32))).astype(BF16)
        return carry

    lax.fori_loop(0, t // NORM_ROWS, norm_tile, 0)
    buf_ref[0:CONV_HALO, :] = buf_ref[t:t + CONV_HALO, :]


def _conv_core(u, z, dw_w, dw_b, ln_g, ln_b, batch, t=256):
    n, c = u.shape
    seq = n // batch
    t = min(t, seq)
    nt = seq // t
    blk = lambda b, i: (b * nt + i, 0)
    whole = lambda b, i: (0, 0)
    w_p = jnp.zeros((CONV_HALO, c), F32).at[:CONV_WIDTH].set(dw_w)
    vec = pl.BlockSpec((1, c), whole)
    return pl.pallas_call(
        _conv_core_kernel,
        grid=(batch, nt),
        in_specs=[pl.BlockSpec((t, c), blk)] * 2 + [pl.BlockSpec(w_p.shape, whole), vec, vec, vec],
        out_specs=pl.BlockSpec((t, c), blk),
        out_shape=jax.ShapeDtypeStruct((n, c), BF16),
        scratch_shapes=[pltpu.VMEM((t + CONV_HALO, c), F32),
                        pltpu.VMEM((SUBLANES - 1, t + CONV_HALO - SUBLANES, CONV_COLS), F32),
                        pltpu.VMEM((t, c), F32)],
        compiler_params=_params("parallel", "arbitrary"),
        name="conv_core",
    )(u, z, w_p, dw_b.reshape(1, c), ln_g.reshape(1, c), ln_b.reshape(1, c))


def _conv_mixer(x, w_in, dw_w, dw_b, ln_g, ln_b, batch):
    w = w_in.astype(BF16)
    u = _mm_glu(x, w, D_INNER)
    z = _mm(x, w, BF16, col0=2 * D_INNER)
    return _conv_core(u, z, dw_w, dw_b, ln_g, ln_b, batch)


def kernel(x, p, positions, gla_w_in, gla_w_a2, gla_b_a, gla_gn_g, gla_w_out, dsa_w_in, dsa_w_out,
           conv_w_in, conv_dw_w, conv_dw_b, conv_ln_g, conv_ln_b, conv_w_out, ln_g, ln_b, ple_w,
           ple_gate_w):
    batch, seq, d = x.shape
    depth = p.shape[0]
    n = batch * seq
    alpha = (2 * depth) ** 0.25
    xf = x.reshape(n, d)
    xb = xf
    pos = positions.astype(F32).reshape(n, 1)
    p_all = p.reshape(depth * n, p.shape[-1])
    for i in range(depth):
        kind, j = i % N_MIXERS, i // N_MIXERS
        if kind == 0:
            u = _gla_mixer(xb, gla_w_in[j], gla_w_a2[j], gla_b_a[j], gla_gn_g[j], batch)
            w_out = gla_w_out[j]
        elif kind == 1:
            u = _dsa_mixer(xb, pos, dsa_w_in[j], batch)
            w_out = dsa_w_out[j]
        else:
            u = _conv_mixer(xb, conv_w_in[j], conv_dw_w[j], conv_dw_b[j], conv_ln_g[j], conv_ln_b[j], batch)
            w_out = conv_w_out[j]
        xf, xb = _post(u, xf, p_all, i, w_out, ln_g[i], ln_b[i], ple_w[i], ple_gate_w[i], alpha)
    return xf.reshape(batch, seq, d)
```

```python
import functools

import jax
import jax.numpy as jnp
from jax import lax
from jax.experimental import pallas as pl
from jax.experimental.pallas import tpu as pltpu

F32 = jnp.float32
BF16 = jnp.bfloat16

D_MODEL = 1024
N_MIXERS = 3
D_INNER = 2 * D_MODEL
GLA_HEADS = 4
GLA_DK = D_MODEL // GLA_HEADS
GLA_DV = D_INNER // GLA_HEADS
GLA_RANK = 16
GLA_GATE_NORM = 16.0
GLA_CHUNK = 64
DSA_HEADS = 16
DSA_HEAD_DIM = D_INNER // DSA_HEADS
IDX_HEADS = 8
IDX_DIM = 64
TOPK_MAX = 256
ATTN_SCALE = DSA_HEAD_DIM ** -0.5
INDEX_SCALE = (IDX_HEADS ** -0.5) * (IDX_DIM ** -0.5)
CONV_WIDTH = 31
ROPE_THETA = 500000.0
DSA_ROT_DIM = DSA_HEAD_DIM // 4
IDX_ROT_DIM = IDX_DIM // 4
PLE_DIM = 256
NORM_EPS = 1e-5

LANES = 128
LOG2E = 1.4426950408889634
INT_MIN = -(2 ** 31)
VMEM_LIMIT = 56 * 1024 * 1024

NT_DIMS = (((1,), (1,)), ((), ()))
TN_DIMS = (((0,), (0,)), ((), ()))


def _params(*sem):
    return pltpu.CompilerParams(dimension_semantics=sem, vmem_limit_bytes=VMEM_LIMIT)


def _dot(a, b):
    return jnp.dot(a, b, preferred_element_type=F32)


def _silu(x):
    return x * jax.nn.sigmoid(x)


def _mm_kernel(x_ref, w_ref, o_ref):
    o_ref[...] = _dot(x_ref[...].astype(BF16), w_ref[...]).astype(o_ref.dtype)


def _mm(x, w, out_dtype, col0=0, cols=None, tm=1024, tn=1024):
    n, k = x.shape
    c = w.shape[1] - col0 if cols is None else cols
    tm, tn = min(tm, n), min(tn, c)
    assert col0 % tn == 0 and c % tn == 0
    j0 = col0 // tn
    return pl.pallas_call(
        _mm_kernel,
        grid=(n // tm, c // tn),
        in_specs=[pl.BlockSpec((tm, k), lambda i, j: (i, 0)),
                  pl.BlockSpec((k, tn), lambda i, j: (0, j0 + j))],
        out_specs=pl.BlockSpec((tm, tn), lambda i, j: (i, j)),
        out_shape=jax.ShapeDtypeStruct((n, c), out_dtype),
        compiler_params=_params("parallel", "parallel"),
        name="proj",
    )(x, w)


def _mm_glu_kernel(x_ref, wa_ref, wg_ref, o_ref):
    xb = x_ref[...].astype(BF16)
    o_ref[...] = _dot(xb, wa_ref[...]) * jax.nn.sigmoid(_dot(xb, wg_ref[...]))


def _mm_glu(x, w, c, tm=1024, tn=1024):
    n, k = x.shape
    tm, tn = min(tm, n), min(tn, c)
    assert c % tn == 0
    nj = c // tn
    return pl.pallas_call(
        _mm_glu_kernel,
        grid=(n // tm, nj),
        in_specs=[pl.BlockSpec((tm, k), lambda i, j: (i, 0)),
                  pl.BlockSpec((k, tn), lambda i, j: (0, j)),
                  pl.BlockSpec((k, tn), lambda i, j: (0, nj + j))],
        out_specs=pl.BlockSpec((tm, tn), lambda i, j: (i, j)),
        out_shape=jax.ShapeDtypeStruct((n, c), F32),
        compiler_params=_params("parallel", "parallel"),
        name="proj_glu",
    )(x, w, w)


def _post_kernel(alpha, u_ref, x_ref, p_ref, wout_ref, lng_ref, lnb_ref, plew_ref, gatew_ref, o_ref, ob_ref):
    y = _dot(u_ref[...], wout_ref[...])
    ple = _dot(p_ref[...].astype(BF16), plew_ref[...])
    r = alpha * x_ref[...] + y
    mu = jnp.mean(r, axis=-1, keepdims=True)
    d = r - mu
    var = jnp.mean(d * d, axis=-1, keepdims=True)
    x1 = d * lax.rsqrt(var + NORM_EPS) * lng_ref[...] + lnb_ref[...]
    gate = _dot(x1.astype(BF16), gatew_ref[...])
    out = x1 + ple * jax.nn.sigmoid(gate)
    o_ref[...] = out
    ob_ref[...] = out.astype(BF16)


def _post(u, x, p_all, layer, w_out, ln_g, ln_b, ple_w, gate_w, alpha, tm=1024):
    n, d = x.shape
    tm = min(tm, n)
    row = lambda i: (i, 0)
    whole = lambda i: (0, 0)
    p0 = layer * (n // tm)
    return pl.pallas_call(
        functools.partial(_post_kernel, alpha),
        grid=(n // tm,),
        in_specs=[pl.BlockSpec((tm, u.shape[1]), row),
                  pl.BlockSpec((tm, d), row),
                  pl.BlockSpec((tm, p_all.shape[1]), lambda i: (p0 + i, 0)),
                  pl.BlockSpec(w_out.shape, whole),
                  pl.BlockSpec((1, d), whole),
                  pl.BlockSpec((1, d), whole),
                  pl.BlockSpec(ple_w.shape, whole),
                  pl.BlockSpec(gate_w.shape, whole)],
        out_specs=[pl.BlockSpec((tm, d), row)] * 2,
        out_shape=[jax.ShapeDtypeStruct((n, d), F32), jax.ShapeDtypeStruct((n, d), BF16)],
        compiler_params=_params("parallel"),
        name="post",
    )(u, x, p_all, w_out.astype(BF16), ln_g.reshape(1, d), ln_b.reshape(1, d),
      ple_w.astype(BF16), gate_w.astype(BF16))


def _split3_bf16(x):
    hi = x.astype(BF16)
    r = x - hi.astype(F32)
    mid = r.astype(BF16)
    lo = (r - mid.astype(F32)).astype(BF16)
    return hi, mid, lo


def _gla_gate_kernel(x_ref, wq_ref, wk_ref, wa_ref, wa2_ref, ba_ref,
                     qt_ref, kt_ref, ks_ref, dec_ref):
    xb = x_ref[...].astype(BF16)
    a = _dot(xb, wa_ref[...])
    pre = _dot(a.astype(BF16), wa2_ref[...]) + ba_ref[...]
    g = (jnp.minimum(pre, 0.0) - jnp.log(1.0 + jnp.exp(-jnp.abs(pre)))) / GLA_GATE_NORM
    ch = GLA_CHUNK
    ri = lax.broadcasted_iota(jnp.int32, (ch, ch), 0)
    ci = lax.broadcasted_iota(jnp.int32, (ch, ch), 1)
    tri = jnp.where(ri >= ci, 1.0, 0.0).astype(BF16)
    chunks = [slice(c * ch, (c + 1) * ch) for c in range(x_ref.shape[0] // ch)]
    cums = []
    for sl in chunks:
        hi, mid, lo = _split3_bf16(g[sl])
        cums.append(_dot(tri, hi) + _dot(tri, mid) + _dot(tri, lo))
    q = _dot(xb, wq_ref[...])
    for sl, cum in zip(chunks, cums):
        qt_ref[sl, :] = (q[sl] * (GLA_DK ** -0.5) * jnp.exp(cum)).astype(BF16)
    k = _dot(xb, wk_ref[...])
    for sl, cum in zip(chunks, cums):
        last = cum[ch - 1:ch]
        kt_ref[sl, :] = (k[sl] * jnp.exp(-cum)).astype(BF16)
        ks_ref[sl, :] = (k[sl] * jnp.exp(last - cum)).astype(BF16)
    dec_ref[...] = jnp.concatenate([jnp.exp(cum[ch - 1:ch]) for cum in cums], axis=0)


def _gla_gate(x, w, wa, wa2, ba, tm=1024):
    n, d = x.shape
    qk = GLA_HEADS * GLA_DK
    row = lambda i: (i, 0)
    whole = lambda i: (0, 0)
    wa_p = jnp.zeros((d, LANES), BF16).at[:, :GLA_RANK].set(wa)
    wa2_p = jnp.zeros((LANES, qk), BF16).at[:GLA_RANK].set(wa2)
    act = jax.ShapeDtypeStruct((n, qk), BF16)
    return pl.pallas_call(
        _gla_gate_kernel,
        grid=(n // tm,),
        in_specs=[pl.BlockSpec((tm, d), row),
                  pl.BlockSpec((d, qk), lambda i: (0, 0)), pl.BlockSpec((d, qk), lambda i: (0, 1)),
                  pl.BlockSpec(wa_p.shape, whole), pl.BlockSpec(wa2_p.shape, whole),
                  pl.BlockSpec((1, qk), whole)],
        out_specs=[pl.BlockSpec((tm, qk), row)] * 3 + [pl.BlockSpec((tm // GLA_CHUNK, qk), row)],
        out_shape=[act, act, act, jax.ShapeDtypeStruct((n // GLA_CHUNK, qk), F32)],
        compiler_params=_params("parallel"),
        name="gla_gate",
    )(x, w, w, wa_p, wa2_p, ba.reshape(1, qk))


def _gla_core_kernel(qt_ref, kt_ref, ks_ref, v_ref, dec_ref, z_ref, gn_ref, o_ref, s_ref, sb_ref, oi_ref):
    @pl.when(pl.program_id(2) == 0)
    def _():
        s_ref[...] = jnp.zeros_like(s_ref)
        sb_ref[...] = jnp.zeros_like(sb_ref)

    ch, dk, dv = GLA_CHUNK, GLA_DK, GLA_DV
    ri = lax.broadcasted_iota(jnp.int32, (ch, ch), 0)
    ci = lax.broadcasted_iota(jnp.int32, (ch, ch), 1)
    tril = ri >= ci
    dec_t = [dec_ref[:, j * dk:(j + 1) * dk].T for j in range(GLA_HEADS_PER_STEP)]
    chunks = [(c, j) for c in range(qt_ref.shape[0] // ch) for j in range(GLA_HEADS_PER_STEP)]

    def tiles(c, j):
        return slice(c * ch, (c + 1) * ch), slice(j * dk, (j + 1) * dk), slice(j * dv, (j + 1) * dv)

    a = {}
    for c, j in chunks:
        sl, kc, _ = tiles(c, j)
        a[c, j] = jnp.where(tril, lax.dot_general(qt_ref[sl, kc], kt_ref[sl, kc], NT_DIMS,
                                                  preferred_element_type=F32), 0.0).astype(BF16)
    for c, j in chunks:
        sl, _, vc = tiles(c, j)
        oi_ref[sl, vc] = _dot(a[c, j], v_ref[sl, vc])
    for c, j in chunks:
        sl, kc, vc = tiles(c, j)
        v = v_ref[sl, vc]
        o = oi_ref[sl, vc] + _dot(qt_ref[sl, kc], sb_ref[j])
        s = s_ref[j] * dec_t[j][:, c:c + 1] + lax.dot_general(ks_ref[sl, kc], v, TN_DIMS, preferred_element_type=F32)
        s_ref[j] = s
        sb_ref[j] = s.astype(BF16)
        o = o * lax.rsqrt(jnp.mean(o * o, axis=-1, keepdims=True) + NORM_EPS)
        o = o * gn_ref[:, vc]
        o_ref[sl, vc] = (o * _silu(z_ref[sl, vc].astype(F32))).astype(BF16)


GLA_HEADS_PER_STEP = 2


def _gla_core(qt, kt, ks, vz, dec, gn_g, batch, tc=1024):
    n = qt.shape[0]
    seq = n // batch
    tc = min(tc, seq)
    nt = seq // tc
    hb = GLA_HEADS_PER_STEP
    groups = GLA_HEADS // hb
    tok = lambda b, g, i: (b * nt + i, g)
    return pl.pallas_call(
        _gla_core_kernel,
        grid=(batch, groups, nt),
        in_specs=[pl.BlockSpec((tc, hb * GLA_DK), tok)] * 3
                 + [pl.BlockSpec((tc, hb * GLA_DV), tok),
                    pl.BlockSpec((tc // GLA_CHUNK, hb * GLA_DK), tok),
                    pl.BlockSpec((tc, hb * GLA_DV), lambda b, g, i: (b * nt + i, groups + g)),
                    pl.BlockSpec((1, hb * GLA_DV), lambda b, g, i: (0, g))],
        out_specs=pl.BlockSpec((tc, hb * GLA_DV), tok),
        out_shape=jax.ShapeDtypeStruct((n, D_INNER), BF16),
        scratch_shapes=[pltpu.VMEM((hb, GLA_DK, GLA_DV), F32), pltpu.VMEM((hb, GLA_DK, GLA_DV), BF16),
                        pltpu.VMEM((tc, hb * GLA_DV), F32)],
        compiler_params=_params("parallel", "parallel", "arbitrary"),
        name="gla_core",
    )(qt, kt, ks, vz, dec, vz, gn_g.reshape(1, D_INNER))


def _gla_mixer(x, w_in, w_a2, b_a, gn_g, batch):
    qk = GLA_HEADS * GLA_DK
    w = w_in.astype(BF16)
    o_v, o_a = 2 * qk, 2 * qk + 2 * D_INNER
    qt, kt, ks, dec = _gla_gate(x, w, w[:, o_a:], w_a2.astype(BF16), b_a)
    vz = _mm(x, w, BF16, col0=o_v, cols=o_a - o_v)
    return _gla_core(qt, kt, ks, vz, dec, gn_g, batch)


def _rope(x, cos, sin_lo, sin_hi, half):
    return (x * cos + pltpu.roll(x, half, 1) * sin_hi
            + pltpu.roll(x, LANES - half, 1) * sin_lo)


def _rope_tables(pos, invf, half, period):
    ang = pos * invf
    cos, sin = jnp.cos(ang), jnp.sin(ang)
    lane = lax.broadcasted_iota(jnp.int32, ang.shape, 1) % period
    sin_lo = jnp.where(lane < half, -sin, 0.0)
    sin_hi = jnp.where((lane >= half) & (lane < 2 * half), sin, 0.0)
    return cos, sin_lo, sin_hi


def _dsa_prep_kernel(x_ref, pos_ref, invh_ref, invi_ref, wq_ref, wkv_ref, wqi_ref, wki_ref, wwi_ref,
                     q_ref, k_ref, v_ref, qi_ref, ki_ref, wi_ref):
    xb = x_ref[...].astype(BF16)
    pos = pos_ref[...]
    hd = DSA_HEAD_DIM
    cos_h, slo_h, shi_h = _rope_tables(pos, invh_ref[...], DSA_ROT_DIM // 2, hd)
    cos_i, slo_i, shi_i = _rope_tables(pos, invi_ref[...], IDX_ROT_DIM // 2, IDX_DIM)
    q = _dot(xb, wq_ref[...])
    for h in range(DSA_HEADS):
        sl = slice(h * hd, (h + 1) * hd)
        q_ref[:, sl] = (_rope(q[:, sl], cos_h, slo_h, shi_h, DSA_ROT_DIM // 2)
                        * (ATTN_SCALE * LOG2E)).astype(BF16)
    kv = _dot(xb, wkv_ref[...])
    k_ref[...] = _rope(kv[:, :hd], cos_h, slo_h, shi_h, DSA_ROT_DIM // 2).astype(BF16)
    v_ref[:, :hd] = kv[:, hd:].astype(BF16)
    v_ref[:, hd:] = jnp.ones((x_ref.shape[0], LANES), BF16)
    qi = _dot(xb, wqi_ref[...])
    for j in range(qi.shape[1] // LANES):
        sl = slice(j * LANES, (j + 1) * LANES)
        qi_ref[:, sl] = _rope(qi[:, sl], cos_i, slo_i, shi_i, IDX_ROT_DIM // 2).astype(BF16)
    ki = _dot(xb, wki_ref[...])
    ki_ref[...] = _rope(ki, cos_i, slo_i, shi_i, IDX_ROT_DIM // 2).astype(BF16)
    wi_ref[...] = _dot(xb, wwi_ref[...])


def _lane_invfreq(rot_dim, period):
    inv = ROPE_THETA ** (-jnp.arange(0, rot_dim, 2, dtype=F32) / rot_dim)
    head = jnp.concatenate([inv, inv, jnp.zeros((period - rot_dim,), F32)])
    return jnp.tile(head, LANES // period).reshape(1, LANES)


def _pad_cols(w, width):
    return jnp.zeros((w.shape[0], width), w.dtype).at[:, :w.shape[1]].set(w)


def _dsa_prep(x, pos, wq, wkv, wqi, wki, wwi, tm=512):
    n, d = x.shape
    tm = min(tm, n)
    row = lambda i: (i, 0)
    whole = lambda i: (0, 0)
    wki_p, wwi_p = _pad_cols(wki, LANES), _pad_cols(wwi, LANES)
    ws = [wq, wkv, wqi, wki_p, wwi_p]
    return pl.pallas_call(
        _dsa_prep_kernel,
        grid=(n // tm,),
        in_specs=[pl.BlockSpec((tm, d), row), pl.BlockSpec((tm, 1), row),
                  pl.BlockSpec((1, LANES), whole), pl.BlockSpec((1, LANES), whole)]
                 + [pl.BlockSpec(w.shape, whole) for w in ws],
        out_specs=[pl.BlockSpec((tm, D_INNER), row), pl.BlockSpec((tm, LANES), row),
                   pl.BlockSpec((tm, 2 * LANES), row), pl.BlockSpec((tm, IDX_HEADS * IDX_DIM), row),
                   pl.BlockSpec((tm, LANES), row), pl.BlockSpec((tm, LANES), row)],
        out_shape=[jax.ShapeDtypeStruct((n, D_INNER), BF16), jax.ShapeDtypeStruct((n, LANES), BF16),
                   jax.ShapeDtypeStruct((n, 2 * LANES), BF16),
                   jax.ShapeDtypeStruct((n, IDX_HEADS * IDX_DIM), BF16),
                   jax.ShapeDtypeStruct((n, LANES), BF16), jax.ShapeDtypeStruct((n, LANES), F32)],
        compiler_params=_params("parallel"),
        name="dsa_prep",
    )(x, pos, _lane_invfreq(DSA_ROT_DIM, DSA_HEAD_DIM), _lane_invfreq(IDX_ROT_DIM, IDX_DIM), *ws)


COUNT_ROWS = 64
MASKED = -1e30
QSUB = 128


def _sortable_key(score):
    b = pltpu.bitcast(score, jnp.int32)
    key = b ^ ((b >> 31) & 0x7FFFFFFF)
    return jnp.where(score == 0.0, 0, key)


def _dsa_core_kernel(topk, tq, sc, hg,
                     q_ref, qi_ref, wi_ref, z_ref, k_ref, va_ref, ki_ref, o_ref,
                     key_ref, hi_ref, lo_ref, kx_ref, qx_ref, s_ref, mx_ref, m_ref, acc_ref):
    qb = pl.program_id(1)
    t0 = qb * tq
    nkc = (t0 + tq + sc - 1) // sc
    hd = DSA_HEAD_DIM
    nsub = tq // QSUB
    groups = DSA_HEADS // hg
    rows = hg * QSUB

    @pl.when(qb == 0)
    def _():
        for a in range(nsub):
            kx_ref[a, :, :hd] = k_ref[...]
        ri = lax.broadcasted_iota(jnp.int32, (QSUB, LANES), 0)
        ci = lax.broadcasted_iota(jnp.int32, (QSUB, LANES), 1)
        eye = jnp.where(ri == ci, 1.0, 0.0).astype(BF16)
        for j in range(nsub * DSA_HEADS):
            qx_ref[j * QSUB:(j + 1) * QSUB, hd:] = eye

    qi_s = jnp.concatenate([qi_ref[:, h * IDX_DIM:(h + 1) * IDX_DIM] for h in range(IDX_HEADS)], axis=0)
    wi_t = wi_ref[...].T
    wi_row = jnp.concatenate([wi_t[h:h + 1, :] for h in range(IDX_HEADS)], axis=1)
    k_pos = lax.broadcasted_iota(jnp.int32, (sc, tq), 0)
    t_pos = t0 + lax.broadcasted_iota(jnp.int32, (sc, tq), 1)

    def run(body):
        lax.fori_loop(0, nkc // 2, lambda i, carry: (body(2 * i, 2), carry)[1], 0)

        @pl.when(nkc % 2 == 1)
        def _():
            body(nkc - 1, 1)

    def score_chunks(c, nc):
        ki_c = ki_ref[pl.ds(pl.multiple_of(c * sc, sc), nc * sc), :][:, :IDX_DIM]
        rel_all = lax.dot_general(ki_c, qi_s, NT_DIMS, preferred_element_type=F32)
        for i in range(nc):
            rel = jnp.maximum(rel_all[i * sc:(i + 1) * sc], 0.0) * wi_row
            score = rel[:, :tq]
            for h in range(1, IDX_HEADS):
                score = score + rel[:, h * tq:(h + 1) * tq]
            score = score * INDEX_SCALE
            key = jnp.where((c + i) * sc + k_pos <= t_pos, _sortable_key(score), INT_MIN)
            key_ref[c + i] = key
            hi_ref[c + i] = (key >> 16).astype(jnp.int16)

    run(score_chunks)

    def count16(ref, pred):
        def body(c, acc):
            m = jnp.where(pred(ref[c]), jnp.bfloat16(1), jnp.bfloat16(0))
            for j in range(sc // COUNT_ROWS):
                acc = acc + m[j * COUNT_ROWS:(j + 1) * COUNT_ROWS]
            return acc
        acc = lax.fori_loop(0, nkc, body, jnp.zeros((COUNT_ROWS, tq), BF16))
        return jnp.sum(acc.astype(F32), axis=0, keepdims=True)

    def search16(ref, need):
        def body(i, r):
            cand = r | (jnp.int32(1) << (15 - i))
            cand16 = (cand - 32768).astype(jnp.int16)
            return jnp.where(count16(ref, lambda v: v >= cand16) >= need, cand, r)
        return lax.fori_loop(0, 16, body, jnp.zeros((1, tq), jnp.int32)) - 32768

    t_hi = search16(hi_ref, topk)
    t_hi16 = t_hi.astype(jnp.int16)
    n_hi_gt = count16(hi_ref, lambda v: v > t_hi16)

    def lo_body(c, carry):
        key = key_ref[c]
        lo_ref[c] = jnp.where((key >> 16) == t_hi, (key & 0xFFFF) - 32768, -32768).astype(jnp.int16)
        return carry

    lax.fori_loop(0, nkc, lo_body, 0)
    t_lo = search16(lo_ref, topk - n_hi_gt)
    thr = (t_hi << 16) | (t_lo + 32768)

    def count(pred_fn):
        def body(c, acc):
            m = jnp.where(pred_fn(key_ref[c], c), 1.0, 0.0)
            return acc + jnp.sum(m.reshape(sc // COUNT_ROWS, COUNT_ROWS, tq), axis=0)
        acc = lax.fori_loop(0, nkc, body, jnp.zeros((COUNT_ROWS, tq), F32))
        return jnp.sum(acc, axis=0, keepdims=True)

    n_gt = count(lambda key, c: key > thr)
    n_ge = count(lambda key, c: key >= thr)
    need = topk - n_gt
    has_thr = thr != INT_MIN
    excess = jnp.max(jnp.where(has_thr, n_ge - n_gt - need, 0.0))

    def tie_limit():
        def body(i, r):
            cand = r | (jnp.int32(1) << (15 - i))
            n = count(lambda key, c: (key == thr) & (c * sc + k_pos < cand))
            return jnp.where(n < need, cand, r)
        return lax.fori_loop(0, 16, body, jnp.zeros((1, tq), jnp.int32))

    limit = lax.cond(excess > 0.0, tie_limit, lambda: jnp.full((1, tq), 2 ** 30, jnp.int32))
    limit = jnp.where(has_thr, limit, -1)

    def mask_body(c, carry):
        key = key_ref[c]
        sel = (key > thr) | ((key == thr) & (c * sc + k_pos <= limit))
        mask = jnp.where(sel, 0.0, MASKED).astype(BF16)
        k0 = pl.multiple_of(c * sc, sc)
        for a in range(nsub):
            kx_ref[a, pl.ds(k0, sc), hd:] = mask[:, a * QSUB:(a + 1) * QSUB]
        return carry

    lax.fori_loop(0, nkc, mask_body, 0)

    for a in range(nsub):
        for h in range(DSA_HEADS):
            j = a * DSA_HEADS + h
            qx_ref[j * QSUB:(j + 1) * QSUB, :hd] = q_ref[a * QSUB:(a + 1) * QSUB, h * hd:(h + 1) * hd]

    def logits(u, c, nc):
        k0 = pl.multiple_of(c * sc, sc)
        s = lax.dot_general(qx_ref[u * rows:(u + 1) * rows, :], kx_ref[u // groups, pl.ds(k0, nc * sc), :],
                            NT_DIMS, preferred_element_type=F32)
        mx = mx_ref[...]
        for i in range(nc):
            s_ref[c + i] = s[:, i * sc:(i + 1) * sc]
        for j in range(nc * sc // LANES):
            mx = jnp.maximum(mx, s[:, j * LANES:(j + 1) * LANES])
        mx_ref[...] = mx

    def probs(c, nc):
        m = m_ref[...]
        return jnp.concatenate([jnp.exp2(s_ref[c + i, :, j * LANES:(j + 1) * LANES] - m)
                                for i in range(nc) for j in range(sc // LANES)], axis=1).astype(BF16)

    def accumulate(c, nc, p):
        acc_ref[...] += _dot(p, va_ref[pl.ds(pl.multiple_of(c * sc, sc), nc * sc), :])

    def merged(u, c, nc):
        accumulate(c, nc, probs(c, nc))
        logits(u, c, nc)

    def start_unit():
        m_ref[...] = jnp.broadcast_to(jnp.max(mx_ref[...], axis=-1, keepdims=True), m_ref.shape)
        mx_ref[...] = jnp.full(mx_ref.shape, -jnp.inf, F32)
        acc_ref[...] = jnp.zeros(acc_ref.shape, F32)

    def finish(u):
        a, g = u // groups, u % groups
        qr = slice(a * QSUB, (a + 1) * QSUB)
        for j in range(hg):
            r, sl = slice(j * QSUB, (j + 1) * QSUB), slice((g * hg + j) * hd, (g * hg + j + 1) * hd)
            o = acc_ref[r, :hd] / acc_ref[r, hd:]
            o_ref[qr, sl] = (o * _silu(z_ref[qr, sl].astype(F32))).astype(BF16)

    units = nsub * groups
    mx_ref[...] = jnp.full(mx_ref.shape, -jnp.inf, F32)
    run(lambda c, nc: logits(0, c, nc))
    for u in range(1, units):
        start_unit()
        run(lambda c, nc, u=u: merged(u, c, nc))
        finish(u - 1)
    start_unit()
    run(lambda c, nc: accumulate(c, nc, probs(c, nc)))
    finish(units - 1)


def _dsa_core(q, qi, wi, z, k, va, ki, batch, tq=256, sc=512, hg=8):
    n = q.shape[0]
    seq = n // batch
    tq, sc = min(tq, seq), min(sc, seq)
    nq = seq // tq
    topk = min(TOPK_MAX, seq // 4)
    assert seq // COUNT_ROWS <= 256 and seq <= 2 ** 16 and tq % QSUB == 0
    stacked = (tq // QSUB) * DSA_HEADS * QSUB
    blk = lambda b, i: (b * nq + i, 0)
    full = lambda b, i: (b, 0)
    return pl.pallas_call(
        functools.partial(_dsa_core_kernel, topk, tq, sc, hg),
        grid=(batch, nq),
        in_specs=[pl.BlockSpec((tq, D_INNER), blk), pl.BlockSpec((tq, IDX_HEADS * IDX_DIM), blk),
                  pl.BlockSpec((tq, LANES), blk), pl.BlockSpec((tq, D_INNER), blk),
                  pl.BlockSpec((seq, LANES), full), pl.BlockSpec((seq, 2 * LANES), full),
                  pl.BlockSpec((seq, LANES), full)],
        out_specs=pl.BlockSpec((tq, D_INNER), blk),
        out_shape=jax.ShapeDtypeStruct((n, D_INNER), BF16),
        scratch_shapes=[pltpu.VMEM((seq // sc, sc, tq), jnp.int32),
                        pltpu.VMEM((seq // sc, sc, tq), jnp.int16),
                        pltpu.VMEM((seq // sc, sc, tq), jnp.int16),
                        pltpu.VMEM((tq // QSUB, seq, 2 * LANES), BF16),
                        pltpu.VMEM((stacked, 2 * LANES), BF16),
                        pltpu.VMEM((seq // sc, hg * QSUB, sc), F32),
                        pltpu.VMEM((hg * QSUB, LANES), F32),
                        pltpu.VMEM((hg * QSUB, LANES), F32),
                        pltpu.VMEM((hg * QSUB, 2 * LANES), F32)],
        compiler_params=_params("parallel", "arbitrary"),
        name="dsa_core",
    )(q, qi, wi, z, k, va, ki)


def _dsa_mixer(x, pos, w_in, batch):
    w = w_in.astype(BF16)
    hd = DSA_HEAD_DIM
    o1 = D_INNER
    o3 = o1 + 2 * hd
    o4 = o3 + D_INNER
    o5 = o4 + IDX_HEADS * IDX_DIM
    o6 = o5 + IDX_DIM
    q, k, va, qi, ki, wi = _dsa_prep(x, pos, w[:, :o1], w[:, o1:o3], w[:, o4:o5], w[:, o5:o6], w[:, o6:])
    z = _mm(x, w[:, o3:o4], BF16)
    return _dsa_core(q, qi, wi, z, k, va, ki, batch)


CONV_HALO = 32
CONV_ROWS = 128
CONV_COLS = 128
SUBLANES = 8
NORM_ROWS = 128


def _conv_core_kernel(u_ref, z_ref, w_ref, b_ref, lng_ref, lnb_ref, o_ref,
                      buf_ref, sh_ref, conv_ref):
    t, ch = u_ref.shape

    @pl.when(pl.program_id(1) == 0)
    def _():
        buf_ref[0:CONV_HALO, :] = jnp.zeros((CONV_HALO, ch), F32)

    buf_ref[CONV_HALO:CONV_HALO + t, :] = u_ref[...]
    first = CONV_HALO - (CONV_WIDTH - 1)
    span = t + CONV_HALO - SUBLANES
    for cb in range(ch // CONV_COLS):
        cols = slice(cb * CONV_COLS, (cb + 1) * CONV_COLS)
        for s in range(1, SUBLANES):
            sh_ref[s - 1] = buf_ref[s:s + span, cols]

        def tile(i, carry, cols=cols):
            r0 = pl.multiple_of(i * CONV_ROWS, CONV_ROWS)
            acc = jnp.zeros((CONV_ROWS, CONV_COLS), F32)
            for j in range(CONV_WIDTH):
                s = (first + j) % SUBLANES
                base = r0 + (first + j - s)
                if s == 0:
                    src = buf_ref[pl.ds(base, CONV_ROWS), cols]
                else:
                    src = sh_ref[s - 1, pl.ds(base, CONV_ROWS), :]
                acc = acc + src * w_ref[j:j + 1, cols]
            conv_ref[pl.ds(r0, CONV_ROWS), cols] = acc + b_ref[:, cols]
            return carry

        lax.fori_loop(0, t // CONV_ROWS, tile, 0)

    def norm_tile(i, carry):
        rows = pl.ds(pl.multiple_of(i * NORM_ROWS, NORM_ROWS), NORM_ROWS)
        u = conv_ref[rows, :]
        mu = jnp.mean(u, axis=-1, keepdims=True)
        d = u - mu
        var = jnp.mean(d * d, axis=-1, keepdims=True)
        u = _silu(d * lax.rsqrt(var + NORM_EPS) * lng_ref[...] + lnb_ref[...])
        o_ref[rows, :] = (u * _silu(z_ref[rows, :].astype(F32))).astype(BF16)
        return carry

    lax.fori_loop(0, t // NORM_ROWS, norm_tile, 0)
    buf_ref[0:CONV_HALO, :] = buf_ref[t:t + CONV_HALO, :]


def _conv_core(u, z, dw_w, dw_b, ln_g, ln_b, batch, t=256):
    n, c = u.shape
    seq = n // batch
    t = min(t, seq)
    nt = seq // t
    blk = lambda b, i: (b * nt + i, 0)
    whole = lambda b, i: (0, 0)
    w_p = jnp.zeros((CONV_HALO, c), F32).at[:CONV_WIDTH].set(dw_w)
    vec = pl.BlockSpec((1, c), whole)
    return pl.pallas_call(
        _conv_core_kernel,
        grid=(batch, nt),
        in_specs=[pl.BlockSpec((t, c), blk)] * 2 + [pl.BlockSpec(w_p.shape, whole), vec, vec, vec],
        out_specs=pl.BlockSpec((t, c), blk),
        out_shape=jax.ShapeDtypeStruct((n, c), BF16),
        scratch_shapes=[pltpu.VMEM((t + CONV_HALO, c), F32),
                        pltpu.VMEM((SUBLANES - 1, t + CONV_HALO - SUBLANES, CONV_COLS), F32),
                        pltpu.VMEM((t, c), F32)],
        compiler_params=_params("parallel", "arbitrary"),
        name="conv_core",
    )(u, z, w_p, dw_b.reshape(1, c), ln_g.reshape(1, c), ln_b.reshape(1, c))


def _conv_mixer(x, w_in, dw_w, dw_b, ln_g, ln_b, batch):
    w = w_in.astype(BF16)
    u = _mm_glu(x, w, D_INNER)
    z = _mm(x, w, BF16, col0=2 * D_INNER)
    return _conv_core(u, z, dw_w, dw_b, ln_g, ln_b, batch)


def kernel(x, p, positions, gla_w_in, gla_w_a2, gla_b_a, gla_gn_g, gla_w_out, dsa_w_in, dsa_w_out,
           conv_w_in, conv_dw_w, conv_dw_b, conv_ln_g, conv_ln_b, conv_w_out, ln_g, ln_b, ple_w,
           ple_gate_w):
    batch, seq, d = x.shape
    depth = p.shape[0]
    n = batch * seq
    alpha = (2 * depth) ** 0.25
    xf = x.reshape(n, d)
    xb = xf
    pos = positions.astype(F32).reshape(n, 1)
    p_all = p.reshape(depth * n, p.shape[-1])
    for i in range(depth):
        kind, j = i % N_MIXERS, i // N_MIXERS
        if kind == 0:
            u = _gla_mixer(xb, gla_w_in[j], gla_w_a2[j], gla_b_a[j], gla_gn_g[j], batch)
            w_out = gla_w_out[j]
        elif kind == 1:
            u = _dsa_mixer(xb, pos, dsa_w_in[j], batch)
            w_out = dsa_w_out[j]
        else:
            u = _conv_mixer(xb, conv_w_in[j], conv_dw_w[j], conv_dw_b[j], conv_ln_g[j], conv_ln_b[j], batch)
            w_out = conv_w_out[j]
        xf, xb = _post(u, xf, p_all, i, w_out, ln_g[i], ln_b[i], ple_w[i], ple_gate_w[i], alpha)
    return xf.reshape(batch, seq, d)
```

```python
import functools

import jax
import jax.numpy as jnp
from jax import lax
from jax.experimental import pallas as pl
from jax.experimental.pallas import tpu as pltpu

F32 = jnp.float32
BF16 = jnp.bfloat16

D_MODEL = 1024
N_MIXERS = 3
D_INNER = 2 * D_MODEL
GLA_HEADS = 4
GLA_DK = D_MODEL // GLA_HEADS
GLA_DV = D_INNER // GLA_HEADS
GLA_RANK = 16
GLA_GATE_NORM = 16.0
GLA_CHUNK = 64
DSA_HEADS = 16
DSA_HEAD_DIM = D_INNER // DSA_HEADS
IDX_HEADS = 8
IDX_DIM = 64
TOPK_MAX = 256
ATTN_SCALE = DSA_HEAD_DIM ** -0.5
INDEX_SCALE = (IDX_HEADS ** -0.5) * (IDX_DIM ** -0.5)
CONV_WIDTH = 31
ROPE_THETA = 500000.0
DSA_ROT_DIM = DSA_HEAD_DIM // 4
IDX_ROT_DIM = IDX_DIM // 4
PLE_DIM = 256
NORM_EPS = 1e-5

LANES = 128
LOG2E = 1.4426950408889634
INT_MIN = -(2 ** 31)
VMEM_LIMIT = 56 * 1024 * 1024

NT_DIMS = (((1,), (1,)), ((), ()))
TN_DIMS = (((0,), (0,)), ((), ()))


def _params(*sem):
    return pltpu.CompilerParams(dimension_semantics=sem, vmem_limit_bytes=VMEM_LIMIT)


def _dot(a, b):
    return jnp.dot(a, b, preferred_element_type=F32)


def _silu(x):
    return x * jax.nn.sigmoid(x)


def _mm_kernel(x_ref, w_ref, o_ref):
    o_ref[...] = _dot(x_ref[...].astype(BF16), w_ref[...]).astype(o_ref.dtype)


def _mm(x, w, out_dtype, col0=0, cols=None, tm=2048, tn=1024):
    n, k = x.shape
    c = w.shape[1] - col0 if cols is None else cols
    tm, tn = min(tm, n), min(tn, c)
    assert col0 % tn == 0 and c % tn == 0
    j0 = col0 // tn
    return pl.pallas_call(
        _mm_kernel,
        grid=(n // tm, c // tn),
        in_specs=[pl.BlockSpec((tm, k), lambda i, j: (i, 0)),
                  pl.BlockSpec((k, tn), lambda i, j: (0, j0 + j))],
        out_specs=pl.BlockSpec((tm, tn), lambda i, j: (i, j)),
        out_shape=jax.ShapeDtypeStruct((n, c), out_dtype),
        compiler_params=_params("parallel", "parallel"),
        name="proj",
    )(x, w)


def _mm_glu_kernel(x_ref, wa_ref, wg_ref, o_ref):
    xb = x_ref[...].astype(BF16)
    o_ref[...] = _dot(xb, wa_ref[...]) * jax.nn.sigmoid(_dot(xb, wg_ref[...]))


def _mm_glu(x, w, c, tm=1024, tn=1024):
    n, k = x.shape
    tm, tn = min(tm, n), min(tn, c)
    assert c % tn == 0
    nj = c // tn
    return pl.pallas_call(
        _mm_glu_kernel,
        grid=(n // tm, nj),
        in_specs=[pl.BlockSpec((tm, k), lambda i, j: (i, 0)),
                  pl.BlockSpec((k, tn), lambda i, j: (0, j)),
                  pl.BlockSpec((k, tn), lambda i, j: (0, nj + j))],
        out_specs=pl.BlockSpec((tm, tn), lambda i, j: (i, j)),
        out_shape=jax.ShapeDtypeStruct((n, c), F32),
        compiler_params=_params("parallel", "parallel"),
        name="proj_glu",
    )(x, w, w)


def _post_kernel(alpha, u_ref, x_ref, p_ref, wout_ref, lng_ref, lnb_ref, plew_ref, gatew_ref, o_ref, ob_ref):
    y = _dot(u_ref[...], wout_ref[...])
    ple = _dot(p_ref[...].astype(BF16), plew_ref[...])
    r = alpha * x_ref[...] + y
    mu = jnp.mean(r, axis=-1, keepdims=True)
    d = r - mu
    var = jnp.mean(d * d, axis=-1, keepdims=True)
    x1 = d * lax.rsqrt(var + NORM_EPS) * lng_ref[...] + lnb_ref[...]
    gate = _dot(x1.astype(BF16), gatew_ref[...])
    out = x1 + ple * jax.nn.sigmoid(gate)
    o_ref[...] = out
    ob_ref[...] = out.astype(BF16)


def _post(u, x, p_all, layer, w_out, ln_g, ln_b, ple_w, gate_w, alpha, tm=1024):
    n, d = x.shape
    tm = min(tm, n)
    row = lambda i: (i, 0)
    whole = lambda i: (0, 0)
    p0 = layer * (n // tm)
    return pl.pallas_call(
        functools.partial(_post_kernel, alpha),
        grid=(n // tm,),
        in_specs=[pl.BlockSpec((tm, u.shape[1]), row),
                  pl.BlockSpec((tm, d), row),
                  pl.BlockSpec((tm, p_all.shape[1]), lambda i: (p0 + i, 0)),
                  pl.BlockSpec(w_out.shape, whole),
                  pl.BlockSpec((1, d), whole),
                  pl.BlockSpec((1, d), whole),
                  pl.BlockSpec(ple_w.shape, whole),
                  pl.BlockSpec(gate_w.shape, whole)],
        out_specs=[pl.BlockSpec((tm, d), row)] * 2,
        out_shape=[jax.ShapeDtypeStruct((n, d), F32), jax.ShapeDtypeStruct((n, d), BF16)],
        compiler_params=_params("parallel"),
        name="post",
    )(u, x, p_all, w_out.astype(BF16), ln_g.reshape(1, d), ln_b.reshape(1, d),
      ple_w.astype(BF16), gate_w.astype(BF16))


def _split3_bf16(x):
    hi = x.astype(BF16)
    r = x - hi.astype(F32)
    mid = r.astype(BF16)
    lo = (r - mid.astype(F32)).astype(BF16)
    return hi, mid, lo


def _gla_gate_kernel(x_ref, wq_ref, wk_ref, wa_ref, wa2_ref, ba_ref,
                     qt_ref, kt_ref, ks_ref, dec_ref):
    xb = x_ref[...].astype(BF16)
    a = _dot(xb, wa_ref[...])
    pre = _dot(a.astype(BF16), wa2_ref[...]) + ba_ref[...]
    g = (jnp.minimum(pre, 0.0) - jnp.log(1.0 + jnp.exp(-jnp.abs(pre)))) / GLA_GATE_NORM
    ch = GLA_CHUNK
    ri = lax.broadcasted_iota(jnp.int32, (ch, ch), 0)
    ci = lax.broadcasted_iota(jnp.int32, (ch, ch), 1)
    tri = jnp.where(ri >= ci, 1.0, 0.0).astype(BF16)
    chunks = [slice(c * ch, (c + 1) * ch) for c in range(x_ref.shape[0] // ch)]
    cums = []
    for sl in chunks:
        hi, mid, lo = _split3_bf16(g[sl])
        cums.append(_dot(tri, hi) + _dot(tri, mid) + _dot(tri, lo))
    q = _dot(xb, wq_ref[...])
    for sl, cum in zip(chunks, cums):
        qt_ref[sl, :] = (q[sl] * (GLA_DK ** -0.5) * jnp.exp(cum)).astype(BF16)
    k = _dot(xb, wk_ref[...])
    for sl, cum in zip(chunks, cums):
        last = cum[ch - 1:ch]
        kt_ref[sl, :] = (k[sl] * jnp.exp(-cum)).astype(BF16)
        ks_ref[sl, :] = (k[sl] * jnp.exp(last - cum)).astype(BF16)
    dec_ref[...] = jnp.concatenate([jnp.exp(cum[ch - 1:ch]) for cum in cums], axis=0)


def _gla_gate(x, w, wa, wa2, ba, tm=1024):
    n, d = x.shape
    qk = GLA_HEADS * GLA_DK
    row = lambda i: (i, 0)
    whole = lambda i: (0, 0)
    wa_p = jnp.zeros((d, LANES), BF16).at[:, :GLA_RANK].set(wa)
    wa2_p = jnp.zeros((LANES, qk), BF16).at[:GLA_RANK].set(wa2)
    act = jax.ShapeDtypeStruct((n, qk), BF16)
    return pl.pallas_call(
        _gla_gate_kernel,
        grid=(n // tm,),
        in_specs=[pl.BlockSpec((tm, d), row),
                  pl.BlockSpec((d, qk), lambda i: (0, 0)), pl.BlockSpec((d, qk), lambda i: (0, 1)),
                  pl.BlockSpec(wa_p.shape, whole), pl.BlockSpec(wa2_p.shape, whole),
                  pl.BlockSpec((1, qk), whole)],
        out_specs=[pl.BlockSpec((tm, qk), row)] * 3 + [pl.BlockSpec((tm // GLA_CHUNK, qk), row)],
        out_shape=[act, act, act, jax.ShapeDtypeStruct((n // GLA_CHUNK, qk), F32)],
        compiler_params=_params("parallel"),
        name="gla_gate",
    )(x, w, w, wa_p, wa2_p, ba.reshape(1, qk))


GLA_HEADS_PER_STEP = 2


def _gla_core_kernel(qt_ref, kt_ref, ks_ref, v_ref, dec_ref, z_ref, gn_ref, o_ref, s_ref, sb_ref, oi_ref):
    @pl.when(pl.program_id(2) == 0)
    def _():
        s_ref[...] = jnp.zeros_like(s_ref)
        sb_ref[...] = jnp.zeros_like(sb_ref)

    ch, dk, dv = GLA_CHUNK, GLA_DK, GLA_DV
    ri = lax.broadcasted_iota(jnp.int32, (ch, ch), 0)
    ci = lax.broadcasted_iota(jnp.int32, (ch, ch), 1)
    tril = ri >= ci
    dec_t = [dec_ref[:, j * dk:(j + 1) * dk].T for j in range(GLA_HEADS_PER_STEP)]
    chunks = [(c, j) for c in range(qt_ref.shape[0] // ch) for j in range(GLA_HEADS_PER_STEP)]

    def tiles(c, j):
        return slice(c * ch, (c + 1) * ch), slice(j * dk, (j + 1) * dk), slice(j * dv, (j + 1) * dv)

    a = {}
    for c, j in chunks:
        sl, kc, _ = tiles(c, j)
        a[c, j] = jnp.where(tril, lax.dot_general(qt_ref[sl, kc], kt_ref[sl, kc], NT_DIMS,
                                                  preferred_element_type=F32), 0.0).astype(BF16)
    for c, j in chunks:
        sl, _, vc = tiles(c, j)
        oi_ref[sl, vc] = _dot(a[c, j], v_ref[sl, vc])
    for c, j in chunks:
        sl, kc, vc = tiles(c, j)
        v = v_ref[sl, vc]
        o = oi_ref[sl, vc] + _dot(qt_ref[sl, kc], sb_ref[j])
        s = s_ref[j] * dec_t[j][:, c:c + 1] + lax.dot_general(ks_ref[sl, kc], v, TN_DIMS, preferred_element_type=F32)
        s_ref[j] = s
        sb_ref[j] = s.astype(BF16)
        o = o * lax.rsqrt(jnp.mean(o * o, axis=-1, keepdims=True) + NORM_EPS)
        o = o * gn_ref[:, vc]
        o_ref[sl, vc] = (o * _silu(z_ref[sl, vc].astype(F32))).astype(BF16)


def _gla_core(qt, kt, ks, vz, dec, gn_g, batch, tc=1024):
    n = qt.shape[0]
    seq = n // batch
    tc = min(tc, seq)
    nt = seq // tc
    hb = GLA_HEADS_PER_STEP
    groups = GLA_HEADS // hb
    tok = lambda b, g, i: (b * nt + i, g)
    return pl.pallas_call(
        _gla_core_kernel,
        grid=(batch, groups, nt),
        in_specs=[pl.BlockSpec((tc, hb * GLA_DK), tok)] * 3
                 + [pl.BlockSpec((tc, hb * GLA_DV), tok),
                    pl.BlockSpec((tc // GLA_CHUNK, hb * GLA_DK), tok),
                    pl.BlockSpec((tc, hb * GLA_DV), lambda b, g, i: (b * nt + i, groups + g)),
                    pl.BlockSpec((1, hb * GLA_DV), lambda b, g, i: (0, g))],
        out_specs=pl.BlockSpec((tc, hb * GLA_DV), tok),
        out_shape=jax.ShapeDtypeStruct((n, D_INNER), BF16),
        scratch_shapes=[pltpu.VMEM((hb, GLA_DK, GLA_DV), F32), pltpu.VMEM((hb, GLA_DK, GLA_DV), BF16),
                        pltpu.VMEM((tc, hb * GLA_DV), F32)],
        compiler_params=_params("parallel", "parallel", "arbitrary"),
        name="gla_core",
    )(qt, kt, ks, vz, dec, vz, gn_g.reshape(1, D_INNER))


def _gla_mixer(x, w_in, w_a2, b_a, gn_g, batch):
    qk = GLA_HEADS * GLA_DK
    w = w_in.astype(BF16)
    o_v, o_a = 2 * qk, 2 * qk + 2 * D_INNER
    qt, kt, ks, dec = _gla_gate(x, w, w[:, o_a:], w_a2.astype(BF16), b_a)
    vz = _mm(x, w, BF16, col0=o_v, cols=o_a - o_v)
    return _gla_core(qt, kt, ks, vz, dec, gn_g, batch)


def _rope(x, cos, sin_lo, sin_hi, half):
    return (x * cos + pltpu.roll(x, half, 1) * sin_hi
            + pltpu.roll(x, LANES - half, 1) * sin_lo)


def _rope_tables(pos, invf, half, period):
    ang = pos * invf
    cos, sin = jnp.cos(ang), jnp.sin(ang)
    lane = lax.broadcasted_iota(jnp.int32, ang.shape, 1) % period
    sin_lo = jnp.where(lane < half, -sin, 0.0)
    sin_hi = jnp.where((lane >= half) & (lane < 2 * half), sin, 0.0)
    return cos, sin_lo, sin_hi


def _dsa_prep_kernel(x_ref, pos_ref, invh_ref, invi_ref, wq_ref, wkv_ref, wqi_ref, wki_ref, wwi_ref,
                     q_ref, k_ref, v_ref, qi_ref, ki_ref, wi_ref):
    xb = x_ref[...].astype(BF16)
    pos = pos_ref[...]
    hd = DSA_HEAD_DIM
    cos_h, slo_h, shi_h = _rope_tables(pos, invh_ref[...], DSA_ROT_DIM // 2, hd)
    cos_i, slo_i, shi_i = _rope_tables(pos, invi_ref[...], IDX_ROT_DIM // 2, IDX_DIM)
    q = _dot(xb, wq_ref[...])
    for h in range(DSA_HEADS):
        sl = slice(h * hd, (h + 1) * hd)
        q_ref[:, sl] = (_rope(q[:, sl], cos_h, slo_h, shi_h, DSA_ROT_DIM // 2)
                        * (ATTN_SCALE * LOG2E)).astype(BF16)
    kv = _dot(xb, wkv_ref[...])
    k_ref[...] = _rope(kv[:, :hd], cos_h, slo_h, shi_h, DSA_ROT_DIM // 2).astype(BF16)
    v_ref[:, :hd] = kv[:, hd:].astype(BF16)
    v_ref[:, hd:] = jnp.ones((x_ref.shape[0], LANES), BF16)
    qi = _dot(xb, wqi_ref[...])
    for j in range(qi.shape[1] // LANES):
        sl = slice(j * LANES, (j + 1) * LANES)
        qi_ref[:, sl] = _rope(qi[:, sl], cos_i, slo_i, shi_i, IDX_ROT_DIM // 2).astype(BF16)
    ki = _dot(xb, wki_ref[...])
    ki_ref[...] = _rope(ki, cos_i, slo_i, shi_i, IDX_ROT_DIM // 2).astype(BF16)
    wi_ref[...] = _dot(xb, wwi_ref[...])


def _lane_invfreq(rot_dim, period):
    inv = ROPE_THETA ** (-jnp.arange(0, rot_dim, 2, dtype=F32) / rot_dim)
    head = jnp.concatenate([inv, inv, jnp.zeros((period - rot_dim,), F32)])
    return jnp.tile(head, LANES // period).reshape(1, LANES)


def _pad_cols(w, width):
    return jnp.zeros((w.shape[0], width), w.dtype).at[:, :w.shape[1]].set(w)


def _dsa_prep(x, pos, wq, wkv, wqi, wki, wwi, tm=1024):
    n, d = x.shape
    tm = min(tm, n)
    row = lambda i: (i, 0)
    whole = lambda i: (0, 0)
    wki_p, wwi_p = _pad_cols(wki, LANES), _pad_cols(wwi, LANES)
    ws = [wq, wkv, wqi, wki_p, wwi_p]
    return pl.pallas_call(
        _dsa_prep_kernel,
        grid=(n // tm,),
        in_specs=[pl.BlockSpec((tm, d), row), pl.BlockSpec((tm, 1), row),
                  pl.BlockSpec((1, LANES), whole), pl.BlockSpec((1, LANES), whole)]
                 + [pl.BlockSpec(w.shape, whole) for w in ws],
        out_specs=[pl.BlockSpec((tm, D_INNER), row), pl.BlockSpec((tm, LANES), row),
                   pl.BlockSpec((tm, 2 * LANES), row), pl.BlockSpec((tm, IDX_HEADS * IDX_DIM), row),
                   pl.BlockSpec((tm, LANES), row), pl.BlockSpec((tm, LANES), row)],
        out_shape=[jax.ShapeDtypeStruct((n, D_INNER), BF16), jax.ShapeDtypeStruct((n, LANES), BF16),
                   jax.ShapeDtypeStruct((n, 2 * LANES), BF16),
                   jax.ShapeDtypeStruct((n, IDX_HEADS * IDX_DIM), BF16),
                   jax.ShapeDtypeStruct((n, LANES), BF16), jax.ShapeDtypeStruct((n, LANES), F32)],
        compiler_params=_params("parallel"),
        name="dsa_prep",
    )(x, pos, _lane_invfreq(DSA_ROT_DIM, DSA_HEAD_DIM), _lane_invfreq(IDX_ROT_DIM, IDX_DIM), *ws)


COUNT_ROWS = 64
MASKED = -1e30
QSUB = 128


def _sortable_key(score):
    b = pltpu.bitcast(score, jnp.int32)
    key = b ^ ((b >> 31) & 0x7FFFFFFF)
    return jnp.where(score == 0.0, 0, key)


def _dsa_core_kernel(topk, tq, sc, hg,
                     q_ref, qi_ref, wi_ref, z_ref, k_ref, va_ref, ki_ref, o_ref,
                     key_ref, hi_ref, lo_ref, kx_ref, qx_ref, s_ref, mx_ref, m_ref, acc_ref):
    qb = pl.program_id(1)
    t0 = qb * tq
    nkc = (t0 + tq + sc - 1) // sc
    hd = DSA_HEAD_DIM
    nsub = tq // QSUB
    groups = DSA_HEADS // hg
    rows = hg * QSUB

    @pl.when(qb == 0)
    def _():
        for a in range(nsub):
            kx_ref[a, :, :hd] = k_ref[...]
        ri = lax.broadcasted_iota(jnp.int32, (QSUB, LANES), 0)
        ci = lax.broadcasted_iota(jnp.int32, (QSUB, LANES), 1)
        eye = jnp.where(ri == ci, 1.0, 0.0).astype(BF16)
        for j in range(nsub * DSA_HEADS):
            qx_ref[j * QSUB:(j + 1) * QSUB, hd:] = eye

    qi_s = jnp.concatenate([qi_ref[:, h * IDX_DIM:(h + 1) * IDX_DIM] for h in range(IDX_HEADS)], axis=0)
    wi_t = wi_ref[...].T
    wi_row = jnp.concatenate([wi_t[h:h + 1, :] for h in range(IDX_HEADS)], axis=1)
    k_pos = lax.broadcasted_iota(jnp.int32, (sc, tq), 0)
    t_pos = t0 + lax.broadcasted_iota(jnp.int32, (sc, tq), 1)

    def run(body):
        lax.fori_loop(0, nkc // 2, lambda i, carry: (body(2 * i, 2), carry)[1], 0)

        @pl.when(nkc % 2 == 1)
        def _():
            body(nkc - 1, 1)

    def score_chunks(c, nc):
        ki_c = ki_ref[pl.ds(pl.multiple_of(c * sc, sc), nc * sc), :][:, :IDX_DIM]
        rel_all = lax.dot_general(ki_c, qi_s, NT_DIMS, preferred_element_type=F32)
        for i in range(nc):
            rel = jnp.maximum(rel_all[i * sc:(i + 1) * sc], 0.0) * wi_row
            score = rel[:, :tq]
            for h in range(1, IDX_HEADS):
                score = score + rel[:, h * tq:(h + 1) * tq]
            score = score * INDEX_SCALE
            key = jnp.where((c + i) * sc + k_pos <= t_pos, _sortable_key(score), INT_MIN)
            key_ref[c + i] = key
            hi_ref[c + i] = (key >> 16).astype(jnp.int16)

    run(score_chunks)

    def count16(ref, pred):
        def body(c, acc):
            m = jnp.where(pred(ref[c]), jnp.bfloat16(1), jnp.bfloat16(0))
            for j in range(sc // COUNT_ROWS):
                acc = acc + m[j * COUNT_ROWS:(j + 1) * COUNT_ROWS]
            return acc
        acc = lax.fori_loop(0, nkc, body, jnp.zeros((COUNT_ROWS, tq), BF16))
        return jnp.sum(acc.astype(F32), axis=0, keepdims=True)

    def search16(ref, need):
        def body(i, r):
            cand = r | (jnp.int32(1) << (15 - i))
            cand16 = (cand - 32768).astype(jnp.int16)
            return jnp.where(count16(ref, lambda v: v >= cand16) >= need, cand, r)
        return lax.fori_loop(0, 16, body, jnp.zeros((1, tq), jnp.int32)) - 32768

    t_hi = search16(hi_ref, topk)
    t_hi16 = t_hi.astype(jnp.int16)
    n_hi_gt = count16(hi_ref, lambda v: v > t_hi16)

    def lo_body(c, carry):
        key = key_ref[c]
        lo_ref[c] = jnp.where((key >> 16) == t_hi, (key & 0xFFFF) - 32768, -32768).astype(jnp.int16)
        return carry

    lax.fori_loop(0, nkc, lo_body, 0)
    t_lo = search16(lo_ref, topk - n_hi_gt)
    thr = (t_hi << 16) | (t_lo + 32768)

    def count(pred_fn):
        def body(c, acc):
            m = jnp.where(pred_fn(key_ref[c], c), 1.0, 0.0)
            return acc + jnp.sum(m.reshape(sc // COUNT_ROWS, COUNT_ROWS, tq), axis=0)
        acc = lax.fori_loop(0, nkc, body, jnp.zeros((COUNT_ROWS, tq), F32))
        return jnp.sum(acc, axis=0, keepdims=True)

    n_gt = count(lambda key, c: key > thr)
    n_ge = count(lambda key, c: key >= thr)
    need = topk - n_gt
    has_thr = thr != INT_MIN
    excess = jnp.max(jnp.where(has_thr, n_ge - n_gt - need, 0.0))

    def tie_limit():
        def body(i, r):
            cand = r | (jnp.int32(1) << (15 - i))
            n = count(lambda key, c: (key == thr) & (c * sc + k_pos < cand))
            return jnp.where(n < need, cand, r)
        return lax.fori_loop(0, 16, body, jnp.zeros((1, tq), jnp.int32))

    limit = lax.cond(excess > 0.0, tie_limit, lambda: jnp.full((1, tq), 2 ** 30, jnp.int32))
    limit = jnp.where(has_thr, limit, -1)

    def mask_body(c, carry):
        key = key_ref[c]
        sel = (key > thr) | ((key == thr) & (c * sc + k_pos <= limit))
        mask = jnp.where(sel, 0.0, MASKED).astype(BF16)
        k0 = pl.multiple_of(c * sc, sc)
        for a in range(nsub):
            kx_ref[a, pl.ds(k0, sc), hd:] = mask[:, a * QSUB:(a + 1) * QSUB]
        return carry

    lax.fori_loop(0, nkc, mask_body, 0)

    for a in range(nsub):
        for h in range(DSA_HEADS):
            j = a * DSA_HEADS + h
            qx_ref[j * QSUB:(j + 1) * QSUB, :hd] = q_ref[a * QSUB:(a + 1) * QSUB, h * hd:(h + 1) * hd]

    def logits(u, c, nc):
        k0 = pl.multiple_of(c * sc, sc)
        s = lax.dot_general(qx_ref[u * rows:(u + 1) * rows, :], kx_ref[u // groups, pl.ds(k0, nc * sc), :],
                            NT_DIMS, preferred_element_type=F32)
        mx = mx_ref[...]
        for i in range(nc):
            s_ref[c + i] = s[:, i * sc:(i + 1) * sc]
        for j in range(nc * sc // LANES):
            mx = jnp.maximum(mx, s[:, j * LANES:(j + 1) * LANES])
        mx_ref[...] = mx

    def probs(c, nc):
        m = m_ref[...]
        return jnp.concatenate([jnp.exp2(s_ref[c + i, :, j * LANES:(j + 1) * LANES] - m)
                                for i in range(nc) for j in range(sc // LANES)], axis=1).astype(BF16)

    def accumulate(c, nc, p):
        acc_ref[...] += _dot(p, va_ref[pl.ds(pl.multiple_of(c * sc, sc), nc * sc), :])

    def merged(u, c, nc):
        accumulate(c, nc, probs(c, nc))
        logits(u, c, nc)

    def start_unit():
        m_ref[...] = jnp.broadcast_to(jnp.max(mx_ref[...], axis=-1, keepdims=True), m_ref.shape)
        mx_ref[...] = jnp.full(mx_ref.shape, -jnp.inf, F32)
        acc_ref[...] = jnp.zeros(acc_ref.shape, F32)

    def finish(u):
        a, g = u // groups, u % groups
        qr = slice(a * QSUB, (a + 1) * QSUB)
        for j in range(hg):
            r, sl = slice(j * QSUB, (j + 1) * QSUB), slice((g * hg + j) * hd, (g * hg + j + 1) * hd)
            o = acc_ref[r, :hd] / acc_ref[r, hd:]
            o_ref[qr, sl] = (o * _silu(z_ref[qr, sl].astype(F32))).astype(BF16)

    units = nsub * groups
    mx_ref[...] = jnp.full(mx_ref.shape, -jnp.inf, F32)
    run(lambda c, nc: logits(0, c, nc))
    for u in range(1, units):
        start_unit()
        run(lambda c, nc, u=u: merged(u, c, nc))
        finish(u - 1)
    start_unit()
    run(lambda c, nc: accumulate(c, nc, probs(c, nc)))
    finish(units - 1)


def _dsa_core(q, qi, wi, z, k, va, ki, batch, tq=256, sc=512, hg=8):
    n = q.shape[0]
    seq = n // batch
    tq, sc = min(tq, seq), min(sc, seq)
    nq = seq // tq
    topk = min(TOPK_MAX, seq // 4)
    assert seq // COUNT_ROWS <= 256 and seq <= 2 ** 16 and tq % QSUB == 0
    stacked = (tq // QSUB) * DSA_HEADS * QSUB
    blk = lambda b, i: (b * nq + i, 0)
    full = lambda b, i: (b, 0)
    return pl.pallas_call(
        functools.partial(_dsa_core_kernel, topk, tq, sc, hg),
        grid=(batch, nq),
        in_specs=[pl.BlockSpec((tq, D_INNER), blk), pl.BlockSpec((tq, IDX_HEADS * IDX_DIM), blk),
                  pl.BlockSpec((tq, LANES), blk), pl.BlockSpec((tq, D_INNER), blk),
                  pl.BlockSpec((seq, LANES), full), pl.BlockSpec((seq, 2 * LANES), full),
                  pl.BlockSpec((seq, LANES), full)],
        out_specs=pl.BlockSpec((tq, D_INNER), blk),
        out_shape=jax.ShapeDtypeStruct((n, D_INNER), BF16),
        scratch_shapes=[pltpu.VMEM((seq // sc, sc, tq), jnp.int32),
                        pltpu.VMEM((seq // sc, sc, tq), jnp.int16),
                        pltpu.VMEM((seq // sc, sc, tq), jnp.int16),
                        pltpu.VMEM((tq // QSUB, seq, 2 * LANES), BF16),
                        pltpu.VMEM((stacked, 2 * LANES), BF16),
                        pltpu.VMEM((seq // sc, hg * QSUB, sc), F32),
                        pltpu.VMEM((hg * QSUB, LANES), F32),
                        pltpu.VMEM((hg * QSUB, LANES), F32),
                        pltpu.VMEM((hg * QSUB, 2 * LANES), F32)],
        compiler_params=_params("parallel", "arbitrary"),
        name="dsa_core",
    )(q, qi, wi, z, k, va, ki)


def _dsa_mixer(x, pos, w_in, batch):
    w = w_in.astype(BF16)
    hd = DSA_HEAD_DIM
    o1 = D_INNER
    o3 = o1 + 2 * hd
    o4 = o3 + D_INNER
    o5 = o4 + IDX_HEADS * IDX_DIM
    o6 = o5 + IDX_DIM
    q, k, va, qi, ki, wi = _dsa_prep(x, pos, w[:, :o1], w[:, o1:o3], w[:, o4:o5], w[:, o5:o6], w[:, o6:])
    z = _mm(x, w[:, o3:o4], BF16)
    return _dsa_core(q, qi, wi, z, k, va, ki, batch)


CONV_HALO = 32
CONV_ROWS = 128
CONV_COLS = 128
SUBLANES = 8
NORM_ROWS = 128


def _conv_core_kernel(u_ref, z_ref, w_ref, b_ref, lng_ref, lnb_ref, o_ref,
                      buf_ref, sh_ref, conv_ref):
    t, ch = u_ref.shape

    @pl.when(pl.program_id(1) == 0)
    def _():
        buf_ref[0:CONV_HALO, :] = jnp.zeros((CONV_HALO, ch), F32)

    buf_ref[CONV_HALO:CONV_HALO + t, :] = u_ref[...]
    first = CONV_HALO - (CONV_WIDTH - 1)
    span = t + CONV_HALO - SUBLANES
    for cb in range(ch // CONV_COLS):
        cols = slice(cb * CONV_COLS, (cb + 1) * CONV_COLS)
        for s in range(1, SUBLANES):
            sh_ref[s - 1] = buf_ref[s:s + span, cols]

        def tile(i, carry, cols=cols):
            r0 = pl.multiple_of(i * CONV_ROWS, CONV_ROWS)
            acc = jnp.zeros((CONV_ROWS, CONV_COLS), F32)
            for j in range(CONV_WIDTH):
                s = (first + j) % SUBLANES
                base = r0 + (first + j - s)
                if s == 0:
                    src = buf_ref[pl.ds(base, CONV_ROWS), cols]
                else:
                    src = sh_ref[s - 1, pl.ds(base, CONV_ROWS), :]
                acc = acc + src * w_ref[j:j + 1, cols]
            conv_ref[pl.ds(r0, CONV_ROWS), cols] = acc + b_ref[:, cols]
            return carry

        lax.fori_loop(0, t // CONV_ROWS, tile, 0)

    def norm_tile(i, carry):
        rows = pl.ds(pl.multiple_of(i * NORM_ROWS, NORM_ROWS), NORM_ROWS)
        u = conv_ref[rows, :]
        mu = jnp.mean(u, axis=-1, keepdims=True)
        d = u - mu
        var = jnp.mean(d * d, axis=-1, keepdims=True)
        u = _silu(d * lax.rsqrt(var + NORM_EPS) * lng_ref[...] + lnb_ref[...])
        o_ref[rows, :] = (u * _silu(z_ref[rows, :].astype(F32))).astype(BF16)
        return carry

    lax.fori_loop(0, t // NORM_ROWS, norm_tile, 0)
    buf_ref[0:CONV_HALO, :] = buf_ref[t:t + CONV_HALO, :]


def _conv_core(u, z, dw_w, dw_b, ln_g, ln_b, batch, t=256):
    n, c = u.shape
    seq = n // batch
    t = min(t, seq)
    nt = seq // t
    blk = lambda b, i: (b * nt + i, 0)
    whole = lambda b, i: (0, 0)
    w_p = jnp.zeros((CONV_HALO, c), F32).at[:CONV_WIDTH].set(dw_w)
    vec = pl.BlockSpec((1, c), whole)
    return pl.pallas_call(
        _conv_core_kernel,
        grid=(batch, nt),
        in_specs=[pl.BlockSpec((t, c), blk)] * 2 + [pl.BlockSpec(w_p.shape, whole), vec, vec, vec],
        out_specs=pl.BlockSpec((t, c), blk),
        out_shape=jax.ShapeDtypeStruct((n, c), BF16),
        scratch_shapes=[pltpu.VMEM((t + CONV_HALO, c), F32),
                        pltpu.VMEM((SUBLANES - 1, t + CONV_HALO - SUBLANES, CONV_COLS), F32),
                        pltpu.VMEM((t, c), F32)],
        compiler_params=_params("parallel", "arbitrary"),
        name="conv_core",
    )(u, z, w_p, dw_b.reshape(1, c), ln_g.reshape(1, c), ln_b.reshape(1, c))


def _conv_mixer(x, w_in, dw_w, dw_b, ln_g, ln_b, batch):
    w = w_in.astype(BF16)
    u = _mm_glu(x, w, D_INNER)
    z = _mm(x, w, BF16, col0=2 * D_INNER)
    return _conv_core(u, z, dw_w, dw_b, ln_g, ln_b, batch)


def kernel(x, p, positions, gla_w_in, gla_w_a2, gla_b_a, gla_gn_g, gla_w_out, dsa_w_in, dsa_w_out,
           conv_w_in, conv_dw_w, conv_dw_b, conv_ln_g, conv_ln_b, conv_w_out, ln_g, ln_b, ple_w,
           ple_gate_w):
    batch, seq, d = x.shape
    depth = p.shape[0]
    n = batch * seq
    alpha = (2 * depth) ** 0.25
    xf = x.reshape(n, d)
    xb = xf
    pos = positions.astype(F32).reshape(n, 1)
    p_all = p.reshape(depth * n, p.shape[-1])
    for i in range(depth):
        kind, j = i % N_MIXERS, i // N_MIXERS
        if kind == 0:
            u = _gla_mixer(xb, gla_w_in[j], gla_w_a2[j], gla_b_a[j], gla_gn_g[j], batch)
            w_out = gla_w_out[j]
        elif kind == 1:
            u = _dsa_mixer(xb, pos, dsa_w_in[j], batch)
            w_out = dsa_w_out[j]
        else:
            u = _conv_mixer(xb, conv_w_in[j], conv_dw_w[j], conv_dw_b[j], conv_ln_g[j], conv_ln_b[j], batch)
            w_out = conv_w_out[j]
        xf, xb = _post(u, xf, p_all, i, w_out, ln_g[i], ln_b[i], ple_w[i], ple_gate_w[i], alpha)
    return xf.reshape(batch, seq, d)
```
